```python
import math
import jax, jax.numpy as jnp
from jax import lax
import numpy as np

D_MODEL = 2048
BATCH = 2
SEQ = 16384
DEPTH = 2

HEAD_DIM = 64
FOX_HEADS = 8
SWA_HEADS = 8
SWA_KV_HEADS = 2
MOBA_HEADS = 8
FOX_WIDTH = FOX_HEADS * HEAD_DIM
SWA_WIDTH = SWA_HEADS * HEAD_DIM
SWA_KV_WIDTH = SWA_KV_HEADS * HEAD_DIM
MOBA_WIDTH = MOBA_HEADS * HEAD_DIM
Q_BLOCK = 128
SWA_WINDOW = 128
MOBA_BLOCK = 256
MOBA_TOPK = 3
MOBA_Q_CHUNK = 64
DEEPNORM_ALPHA = (2.0 * DEPTH) ** 0.25
DEEPNORM_BETA = (8.0 * DEPTH) ** -0.25
LN_EPS = 1e-5
FORGET_BIAS_INIT = 2.0
NEG_INF = -1e30

IN_SEGMENTS = (
    ("fox_q", FOX_WIDTH), ("fox_k", FOX_WIDTH), ("fox_v", FOX_WIDTH), ("fox_z", FOX_WIDTH), ("fox_f", FOX_HEADS),
    ("swa_q", SWA_WIDTH), ("swa_k", SWA_KV_WIDTH), ("swa_v", SWA_KV_WIDTH), ("swa_z", SWA_WIDTH),
    ("moba_q", MOBA_WIDTH), ("moba_k", MOBA_WIDTH), ("moba_v", MOBA_WIDTH), ("moba_z", MOBA_WIDTH),
    ("gate_fox", D_MODEL), ("gate_swa", D_MODEL), ("gate_moba", D_MODEL),
)
N_IN = sum(size for _, size in IN_SEGMENTS)

kernel_name = "hybrid_fox_swa_moba_gated_deepnorm"


def _split_columns(h):
    parts = {}
    start = 0
    for name, size in IN_SEGMENTS:
        parts[name] = h[..., start:start + size]
        start += size
    return parts


def _heads(t, n_heads):
    b, s, _ = t.shape
    return t.reshape(b, s, n_heads, HEAD_DIM).transpose(0, 2, 1, 3)


def _merge_heads(t):
    b, n, s, d = t.shape
    return t.transpose(0, 2, 1, 3).reshape(b, s, n * d)


def _alibi_slopes(n):
    return jnp.power(2.0, -8.0 * jnp.arange(1, n + 1, dtype=jnp.float32) / n)


def layer_norm(x, g, b):
    xf = x.astype(jnp.float32)
    mu = jnp.mean(xf, axis=-1, keepdims=True)
    var = jnp.mean(jnp.square(xf - mu), axis=-1, keepdims=True)
    return ((xf - mu) * lax.rsqrt(var + LN_EPS) * g + b).astype(x.dtype)


def fox_attention(q, k, v, f_logit):
    b, h, s, d = q.shape
    log_f = jax.nn.log_sigmoid(f_logit.astype(jnp.float32))
    c = jnp.cumsum(log_f, axis=1).transpose(0, 2, 1)
    nblk = s // Q_BLOCK
    q_blocks = q.reshape(b, h, nblk, Q_BLOCK, d).transpose(2, 0, 1, 3, 4)
    c_blocks = c.reshape(b, h, nblk, Q_BLOCK).transpose(2, 0, 1, 3)
    key_pos = jnp.arange(s)
    scale = HEAD_DIM ** -0.5

    def one_block(args):
        i, qb, cb = args
        q_pos = i * Q_BLOCK + jnp.arange(Q_BLOCK)
        logits = jnp.einsum('bhqd,bhkd->bhqk', qb, k, preferred_element_type=jnp.float32) * scale
        logits = logits + cb[..., :, None] - c[:, :, None, :]
        causal = key_pos[None, :] <= q_pos[:, None]
        logits = jnp.where(causal, logits, NEG_INF)
        p = jax.nn.softmax(logits, axis=-1).astype(v.dtype)
        return jnp.einsum('bhqk,bhkd->bhqd', p, v)

    out = lax.map(one_block, (jnp.arange(nblk), q_blocks, c_blocks))
    return out.transpose(1, 2, 0, 3, 4).reshape(b, h, s, d)


def swa_attention(q, k, v, sinks, slopes):
    b, hq, s, d = q.shape
    hkv = k.shape[1]
    g = hq // hkv
    nblk = s // Q_BLOCK
    qb = q.reshape(b, hkv, g, nblk, Q_BLOCK, d)
    kb = k.reshape(b, hkv, nblk, Q_BLOCK, d)
    vb = v.reshape(b, hkv, nblk, Q_BLOCK, d)
    k_prev = jnp.concatenate([jnp.zeros_like(kb[:, :, :1]), kb[:, :, :-1]], axis=2)
    v_prev = jnp.concatenate([jnp.zeros_like(vb[:, :, :1]), vb[:, :, :-1]], axis=2)
    k_band = jnp.concatenate([k_prev, kb], axis=3)
    v_band = jnp.concatenate([v_prev, vb], axis=3)
    logits = jnp.einsum('bhgnqd,bhnkd->bhgnqk', qb, k_band,
                        preferred_element_type=jnp.float32) * (HEAD_DIM ** -0.5)
    rel = (Q_BLOCK + jnp.arange(Q_BLOCK))[:, None] - jnp.arange(2 * Q_BLOCK)[None, :]
    in_window = (rel >= 0) & (rel < SWA_WINDOW)
    real_key = (jnp.arange(nblk)[:, None, None] > 0) | (jnp.arange(2 * Q_BLOCK)[None, None, :] >= Q_BLOCK)
    valid = in_window[None] & real_key
    logits = logits - slopes.reshape(hkv, g)[None, :, :, None, None, None] * rel.astype(jnp.float32)
    logits = jnp.where(valid, logits, NEG_INF)
    sink = sinks.astype(jnp.float32).reshape(hkv, g)[None, :, :, None, None, None]
    m = jnp.maximum(jnp.max(logits, axis=-1, keepdims=True), sink)
    e = jnp.exp(logits - m)
    p = (e / (jnp.sum(e, axis=-1, keepdims=True) + jnp.exp(sink - m))).astype(v.dtype)
    out = jnp.einsum('bhgnqk,bhnkd->bhgnqd', p, v_band)
    return out.reshape(b, hq, s, d)


def moba_attention(q, k, v, slopes):
    b, h, s, d = q.shape
    s_pad = -(-s // MOBA_BLOCK) * MOBA_BLOCK
    pad = ((0, 0), (0, 0), (0, s_pad - s), (0, 0))
    q, k, v = jnp.pad(q, pad), jnp.pad(k, pad), jnp.pad(v, pad)
    nb = s_pad // MOBA_BLOCK
    kb = k.reshape(b, h, nb, MOBA_BLOCK, d)
    vb = v.reshape(b, h, nb, MOBA_BLOCK, d)
    k_mean = jnp.mean(kb.astype(jnp.float32), axis=3).astype(k.dtype)
    top = min(MOBA_TOPK, nb)
    nchunk = s_pad // MOBA_Q_CHUNK
    q_chunks = q.reshape(b, h, nchunk, MOBA_Q_CHUNK, d).transpose(2, 0, 1, 3, 4)
    bi = jnp.arange(b)[:, None, None, None]
    hi = jnp.arange(h)[None, :, None, None]
    offs = jnp.arange(MOBA_BLOCK)
    scale = HEAD_DIM ** -0.5

    def one_chunk(args):
        i, qc = args
        q_pos = i * MOBA_Q_CHUNK + jnp.arange(MOBA_Q_CHUNK)
        own = (i * MOBA_Q_CHUNK) // MOBA_BLOCK
        gate = jnp.einsum('bhqd,bhnd->bhqn', qc, k_mean, preferred_element_type=jnp.float32)
        past = jnp.arange(nb)[None, :] < (q_pos // MOBA_BLOCK)[:, None]
        gate = jnp.where(past, gate, NEG_INF)
        _, idx = lax.top_k(gate, top)
        sel_valid = idx < own
        k_sel = kb[bi, hi, idx]
        v_sel = vb[bi, hi, idx]
        l_sel = jnp.einsum('bhqd,bhqnkd->bhqnk', qc, k_sel, preferred_element_type=jnp.float32) * scale
        sel_pos = idx[..., None] * MOBA_BLOCK + offs
        dist_sel = (q_pos[None, None, :, None, None] - sel_pos).astype(jnp.float32)
        l_sel = l_sel - slopes[None, :, None, None, None] * dist_sel
        l_sel = jnp.where(sel_valid[..., None], l_sel, NEG_INF).reshape(b, h, MOBA_Q_CHUNK, top * MOBA_BLOCK)
        k_own = lax.dynamic_slice_in_dim(kb, own, 1, axis=2)[:, :, 0]
        v_own = lax.dynamic_slice_in_dim(vb, own, 1, axis=2)[:, :, 0]
        rel = q_pos[:, None] - (own * MOBA_BLOCK + offs)[None, :]
        l_own = jnp.einsum('bhqd,bhkd->bhqk', qc, k_own, preferred_element_type=jnp.float32) * scale
        l_own = l_own - slopes[None, :, None, None] * rel.astype(jnp.float32)
        l_own = jnp.where(rel >= 0, l_own, NEG_INF)
        p = jax.nn.softmax(jnp.concatenate([l_sel, l_own], axis=-1), axis=-1).astype(v.dtype)
        p_sel = p[..., :top * MOBA_BLOCK].reshape(b, h, MOBA_Q_CHUNK, top, MOBA_BLOCK)
        p_own = p[..., top * MOBA_BLOCK:]
        return (jnp.einsum('bhqnk,bhqnkd->bhqd', p_sel, v_sel)
                + jnp.einsum('bhqk,bhkd->bhqd', p_own, v_own))

    out = lax.map(one_chunk, (jnp.arange(nchunk), q_chunks))
    out = out.transpose(1, 2, 0, 3, 4).reshape(b, h, s_pad, d)
    return out[:, :, :s]


def hybrid_layer(x, w_in, b_in, sinks, w_br_fox, w_br_swa, w_br_moba, w_out, ln_g, ln_b):
    h = jnp.einsum('bsd,dn->bsn', x, w_in) + b_in
    p = _split_columns(h)
    o_fox = fox_attention(_heads(p["fox_q"], FOX_HEADS), _heads(p["fox_k"], FOX_HEADS),
                          _heads(p["fox_v"], FOX_HEADS), p["fox_f"])
    o_swa = swa_attention(_heads(p["swa_q"], SWA_HEADS), _heads(p["swa_k"], SWA_KV_HEADS),
                          _heads(p["swa_v"], SWA_KV_HEADS), sinks, _alibi_slopes(SWA_HEADS))
    o_moba = moba_attention(_heads(p["moba_q"], MOBA_HEADS), _heads(p["moba_k"], MOBA_HEADS),
                            _heads(p["moba_v"], MOBA_HEADS), _alibi_slopes(MOBA_HEADS))
    br_fox = jnp.einsum('bsw,wd->bsd', _merge_heads(o_fox) * jax.nn.silu(p["fox_z"]), w_br_fox)
    br_swa = jnp.einsum('bsw,wd->bsd', _merge_heads(o_swa) * jax.nn.silu(p["swa_z"]), w_br_swa)
    br_moba = jnp.einsum('bsw,wd->bsd', _merge_heads(o_moba) * jax.nn.silu(p["moba_z"]), w_br_moba)
    y = (jax.nn.sigmoid(p["gate_fox"]) * br_fox + jax.nn.sigmoid(p["gate_swa"]) * br_swa
         + jax.nn.sigmoid(p["gate_moba"]) * br_moba)
    out = jnp.einsum('bsd,de->bse', y, w_out)
    return layer_norm(DEEPNORM_ALPHA * x + out, ln_g, ln_b)


def setup_inputs(seed: int = 0) -> dict:
    key = jax.random.key(seed)
    ks = jax.random.split(key, 11)
    f32 = jnp.float32
    x = jax.random.normal(ks[0], (BATCH, SEQ, D_MODEL), f32)
    col_scale = jnp.concatenate([jnp.full((size,), DEEPNORM_BETA if name.endswith("_v") else 1.0, f32)
                                 for name, size in IN_SEGMENTS])
    forget_offset = jnp.concatenate([jnp.full((size,), FORGET_BIAS_INIT if name == "fox_f" else 0.0, f32)
                                     for name, size in IN_SEGMENTS])
    w_in = jax.random.normal(ks[1], (DEPTH, D_MODEL, N_IN), f32) * (D_MODEL ** -0.5) * col_scale
    b_in = 0.02 * jax.random.normal(ks[2], (DEPTH, N_IN), f32) + forget_offset
    swa_sinks = 0.5 * jax.random.normal(ks[3], (DEPTH, SWA_HEADS), f32)
    w_branch_fox = jax.random.normal(ks[4], (DEPTH, FOX_WIDTH, D_MODEL), f32) * (FOX_WIDTH ** -0.5) * DEEPNORM_BETA
    w_branch_swa = jax.random.normal(ks[5], (DEPTH, SWA_WIDTH, D_MODEL), f32) * (SWA_WIDTH ** -0.5) * DEEPNORM_BETA
    w_branch_moba = jax.random.normal(ks[6], (DEPTH, MOBA_WIDTH, D_MODEL), f32) * (MOBA_WIDTH ** -0.5) * DEEPNORM_BETA
    w_out = jax.random.normal(ks[7], (DEPTH, D_MODEL, D_MODEL), f32) * (D_MODEL ** -0.5) * DEEPNORM_BETA
    ln_gain = 1.0 + 0.02 * jax.random.normal(ks[8], (DEPTH, D_MODEL), f32)
    ln_bias = 0.02 * jax.random.normal(ks[9], (DEPTH, D_MODEL), f32)
    return {"x": x, "w_in": w_in, "b_in": b_in, "swa_sinks": swa_sinks,
            "w_branch_fox": w_branch_fox, "w_branch_swa": w_branch_swa, "w_branch_moba": w_branch_moba,
            "w_out": w_out, "ln_gain": ln_gain, "ln_bias": ln_bias}


def reference(x, w_in, b_in, swa_sinks, w_branch_fox, w_branch_swa, w_branch_moba, w_out, ln_gain, ln_bias):
    for l in range(DEPTH):
        x = hybrid_layer(x, w_in[l], b_in[l], swa_sinks[l], w_branch_fox[l], w_branch_swa[l],
                         w_branch_moba[l], w_out[l], ln_gain[l], ln_bias[l])
    return x
```

```python
import functools

import jax
import jax.numpy as jnp
from jax import lax
from jax.experimental import pallas as pl
from jax.experimental.pallas import tpu as pltpu

D_MODEL = 2048
BATCH = 2
SEQ = 16384
DEPTH = 2
HEAD_DIM = 64
N_HEADS = 8
SWA_KV_HEADS = 2
SWA_GROUP = N_HEADS // SWA_KV_HEADS
WIDTH = N_HEADS * HEAD_DIM
SWA_WINDOW = 128
MOBA_BLOCK = 256
MOBA_TOPK = 3
N_MOBA_BLOCKS = SEQ // MOBA_BLOCK
DEEPNORM_ALPHA = (2.0 * DEPTH) ** 0.25
LN_EPS = 1e-5
NEG_INF = -1e30
SCALE = HEAD_DIM ** -0.5

_SEG = (("fox_q", WIDTH), ("fox_k", WIDTH), ("fox_v", WIDTH), ("fox_z", WIDTH), ("fox_f", N_HEADS),
        ("swa_q", WIDTH), ("swa_k", SWA_KV_HEADS * HEAD_DIM), ("swa_v", SWA_KV_HEADS * HEAD_DIM), ("swa_z", WIDTH),
        ("moba_q", WIDTH), ("moba_k", WIDTH), ("moba_v", WIDTH), ("moba_z", WIDTH),
        ("gate_fox", D_MODEL), ("gate_swa", D_MODEL), ("gate_moba", D_MODEL))
_OFF = {}
_start = 0
for _name, _size in _SEG:
    _OFF[_name] = (_start, _start + _size)
    _start += _size

QKV_ORDER = ("fox_q", "fox_k", "fox_v", "swa_q", "swa_k", "swa_v", "moba_q", "moba_k", "moba_v")
HEAD_SLOT = {}
_slot = 0
for _name in QKV_ORDER:
    HEAD_SLOT[_name] = _slot
    _slot += (_OFF[_name][1] - _OFF[_name][0]) // HEAD_DIM
N_QKV_HEADS = _slot
N_QKV = N_QKV_HEADS * HEAD_DIM
N_ZG = 3 * WIDTH + 3 * D_MODEL
F_PAD = 128

ROWS = BATCH * SEQ
PROJ_TM = 1024
QKV_TN = 768
ZG_TN = 768
FOX_T = 512
SWA_TQ = 512
MERGE_TM = 256
VMEM_LIMIT = 56 * 1024 * 1024

F32 = jnp.float32
BF16 = jnp.bfloat16


def _dot(a, b):
    return jnp.dot(a, b, preferred_element_type=F32)


def _dot_nt(a, b):
    return lax.dot_general(a, b, (((1,), (1,)), ((), ())), preferred_element_type=F32)


def _sigmoid(v):
    return 1.0 / (1.0 + jnp.exp(-v))


def _qkv_proj_kernel(x_ref, w_ref, b_ref, wf_ref, bf_ref, o_ref, f_ref, xb_ref):
    j = pl.program_id(1)

    @pl.when(j == 0)
    def _():
        xb_ref[...] = x_ref[...].astype(BF16)
        f_ref[...] = _dot(xb_ref[...], wf_ref[...]) + bf_ref[...]

    acc = _dot(xb_ref[...], w_ref[...]) + b_ref[...]
    for hh in range(QKV_TN // HEAD_DIM):
        o_ref[0, hh] = acc[:, hh * HEAD_DIM:(hh + 1) * HEAD_DIM].astype(BF16)


def _qkv_proj(x2, w, b, wf, bf):
    tiles_per_batch = SEQ // PROJ_TM
    heads_per_step = QKV_TN // HEAD_DIM
    return pl.pallas_call(
        _qkv_proj_kernel,
        grid=(ROWS // PROJ_TM, N_QKV // QKV_TN),
        in_specs=[
            pl.BlockSpec((PROJ_TM, D_MODEL), lambda i, j: (i, 0)),
            pl.BlockSpec((D_MODEL, QKV_TN), lambda i, j: (0, j)),
            pl.BlockSpec((1, QKV_TN), lambda i, j: (0, j)),
            pl.BlockSpec((D_MODEL, F_PAD), lambda i, j: (0, 0)),
            pl.BlockSpec((1, F_PAD), lambda i, j: (0, 0)),
        ],
        out_specs=[
            pl.BlockSpec((1, heads_per_step, PROJ_TM, HEAD_DIM),
                         lambda i, j: (i // tiles_per_batch, j, i % tiles_per_batch, 0)),
            pl.BlockSpec((PROJ_TM, F_PAD), lambda i, j: (i, 0)),
        ],
        out_shape=[
            jax.ShapeDtypeStruct((BATCH, N_QKV_HEADS, SEQ, HEAD_DIM), BF16),
            jax.ShapeDtypeStruct((ROWS, F_PAD), F32),
        ],
        scratch_shapes=[pltpu.VMEM((PROJ_TM, D_MODEL), BF16)],
        compiler_params=pltpu.CompilerParams(
            dimension_semantics=("arbitrary", "arbitrary"), vmem_limit_bytes=VMEM_LIMIT),
        name="qkv_proj",
    )(x2, w, b, wf, bf)


def _zg_proj_kernel(x_ref, w_ref, b_ref, o_ref, xb_ref):
    j = pl.program_id(1)

    @pl.when(j == 0)
    def _():
        xb_ref[...] = x_ref[...].astype(BF16)

    acc = _dot(xb_ref[...], w_ref[...]) + b_ref[...]
    sig = _sigmoid(acc)

    @pl.when(j < (3 * WIDTH) // ZG_TN)
    def _():
        o_ref[...] = (acc * sig).astype(BF16)

    @pl.when(j >= (3 * WIDTH) // ZG_TN)
    def _():
        o_ref[...] = sig.astype(BF16)


def _zg_proj(x2, w, b):
    return pl.pallas_call(
        _zg_proj_kernel,
        grid=(ROWS // PROJ_TM, N_ZG // ZG_TN),
        in_specs=[
            pl.BlockSpec((PROJ_TM, D_MODEL), lambda i, j: (i, 0)),
            pl.BlockSpec((D_MODEL, ZG_TN), lambda i, j: (0, j)),
            pl.BlockSpec((1, ZG_TN), lambda i, j: (0, j)),
        ],
        out_specs=pl.BlockSpec((PROJ_TM, ZG_TN), lambda i, j: (i, j)),
        out_shape=jax.ShapeDtypeStruct((ROWS, N_ZG), BF16),
        scratch_shapes=[pltpu.VMEM((PROJ_TM, D_MODEL), BF16)],
        compiler_params=pltpu.CompilerParams(
            dimension_semantics=("arbitrary", "arbitrary"), vmem_limit_bytes=VMEM_LIMIT),
        name="zg_proj",
    )(x2, w, b)


def _forget_cumsum_kernel(f_ref, ccol_ref, crow_ref, carry_ref):
    t = pl.program_id(1)

    @pl.when(t == 0)
    def _():
        carry_ref[...] = jnp.zeros_like(carry_ref)

    f = f_ref[...]
    log_f = jnp.minimum(f, 0.0) - jnp.log(1.0 + jnp.exp(-jnp.abs(f)))
    row = lax.broadcasted_iota(jnp.int32, (FOX_T, FOX_T), 0)
    col = lax.broadcasted_iota(jnp.int32, (FOX_T, FOX_T), 1)
    tri = (row >= col).astype(BF16)
    hi = log_f.astype(BF16)
    rem = log_f - hi.astype(F32)
    mid = rem.astype(BF16)
    lo = (rem - mid.astype(F32)).astype(BF16)
    cum = _dot(tri, hi) + _dot(tri, mid) + _dot(tri, lo) + carry_ref[...]
    carry_ref[...] = cum[FOX_T - 1:FOX_T, :]
    ccol_ref[0] = cum
    crow_ref[0, 0] = cum.T[0:N_HEADS, :]


def _forget_cumsum(f_raw):
    nt = SEQ // FOX_T
    return pl.pallas_call(
        _forget_cumsum_kernel,
        grid=(BATCH, nt),
        in_specs=[pl.BlockSpec((FOX_T, F_PAD), lambda b, t: (b * nt + t, 0))],
        out_specs=[
            pl.BlockSpec((1, FOX_T, F_PAD), lambda b, t: (b, t, 0)),
            pl.BlockSpec((1, 1, N_HEADS, FOX_T), lambda b, t: (b, t, 0, 0)),
        ],
        out_shape=[
            jax.ShapeDtypeStruct((BATCH, SEQ, F_PAD), F32),
            jax.ShapeDtypeStruct((BATCH, nt, N_HEADS, FOX_T), F32),
        ],
        scratch_shapes=[pltpu.VMEM((1, F_PAD), F32)],
        compiler_params=pltpu.CompilerParams(dimension_semantics=("arbitrary", "arbitrary")),
        name="forget_cumsum",
    )(f_raw)


def _online_softmax_step(s, vs, carry):
    m, l, acc = carry
    m_new = jnp.maximum(m, jnp.max(s, axis=1, keepdims=True))
    alpha = jnp.exp(m - m_new)
    p = jnp.exp(s - m_new)
    l = alpha * l + jnp.sum(p, axis=1, keepdims=True)
    acc = alpha * acc + _dot(p.astype(BF16), vs)
    return m_new, l, acc


def _softmax_init(rows):
    return (jnp.full((rows, 1), NEG_INF, F32), jnp.zeros((rows, 1), F32), jnp.zeros((rows, HEAD_DIM), F32))


def _fox_kernel(q_ref, k_ref, v_ref, ccol_ref, crow_ref, o_ref):
    hp = pl.program_id(1)
    i = pl.program_id(2)
    t = FOX_T
    lane = lax.broadcasted_iota(jnp.int32, (t, F_PAD), 1)
    row = lax.broadcasted_iota(jnp.int32, (t, t), 0)
    col = lax.broadcasted_iota(jnp.int32, (t, t), 1)
    c_tile = ccol_ref[0]
    for hh in range(2):
        h = hp * 2 + hh
        q = q_ref[0, hh] * jnp.asarray(SCALE, BF16)
        cq = jnp.sum(jnp.where(lane == h, c_tile, 0.0), axis=1, keepdims=True)

        def logits(j):
            start = pl.multiple_of(j * t, t)
            ks = k_ref[0, hh, pl.ds(start, t), :]
            vs = v_ref[0, hh, pl.ds(start, t), :]
            ck = crow_ref[0, j, pl.ds(h, 1), :]
            return _dot_nt(q, ks) + (cq - ck), vs

        def past_tile(j, carry):
            s, vs = logits(j)
            return _online_softmax_step(s, vs, carry)

        carry = lax.fori_loop(0, i, past_tile, _softmax_init(t))
        s, vs = logits(i)
        s = jnp.where(col <= row, s, NEG_INF)
        _, l, acc = _online_softmax_step(s, vs, carry)
        o_ref[0, :, hh * HEAD_DIM:(hh + 1) * HEAD_DIM] = (acc / l).astype(BF16)


def _fox_attention(qkv, ccol, crow):
    nq = SEQ // FOX_T
    q0, k0, v0 = (HEAD_SLOT[n] // 2 for n in ("fox_q", "fox_k", "fox_v"))
    return pl.pallas_call(
        _fox_kernel,
        grid=(BATCH, N_HEADS // 2, nq),
        in_specs=[
            pl.BlockSpec((1, 2, FOX_T, HEAD_DIM), lambda b, hp, i: (b, q0 + hp, i, 0)),
            pl.BlockSpec((1, 2, SEQ, HEAD_DIM), lambda b, hp, i: (b, k0 + hp, 0, 0)),
            pl.BlockSpec((1, 2, SEQ, HEAD_DIM), lambda b, hp, i: (b, v0 + hp, 0, 0)),
            pl.BlockSpec((1, FOX_T, F_PAD), lambda b, hp, i: (b, i, 0)),
            pl.BlockSpec((1, nq, N_HEADS, FOX_T), lambda b, hp, i: (b, 0, 0, 0)),
        ],
        out_specs=pl.BlockSpec((1, FOX_T, 2 * HEAD_DIM), lambda b, hp, i: (b, i, hp)),
        out_shape=jax.ShapeDtypeStruct((BATCH, SEQ, WIDTH), BF16),
        compiler_params=pltpu.CompilerParams(
            dimension_semantics=("arbitrary", "arbitrary", "arbitrary"), vmem_limit_bytes=VMEM_LIMIT),
        name="fox_attention",
    )(qkv, qkv, qkv, ccol, crow)


def _swa_kernel(sinks_ref, slopes_ref, q_ref, k_ref, v_ref, o_ref):
    hkv = pl.program_id(1)
    i = pl.program_id(2)
    w = SWA_WINDOW
    qi = lax.broadcasted_iota(jnp.int32, (w, 2 * w), 0)
    ki = lax.broadcasted_iota(jnp.int32, (w, 2 * w), 1)
    for sub in range(SWA_TQ // w):
        q_start = i * SWA_TQ + sub * w
        k_start = pl.multiple_of(jnp.maximum(q_start - w, 0), w)
        ks = k_ref[0, 0, pl.ds(k_start, 2 * w), :]
        vs = v_ref[0, 0, pl.ds(k_start, 2 * w), :]
        rel = (q_start - k_start) + qi - ki
        valid = (rel >= 0) & (rel < w)
        rel_f = rel.astype(F32)
        for g in range(SWA_GROUP):
            h = hkv * SWA_GROUP + g
            q = q_ref[0, g, sub * w:(sub + 1) * w, :] * jnp.asarray(SCALE, BF16)
            s = _dot_nt(q, ks) - slopes_ref[h] * rel_f
            s = jnp.where(valid, s, NEG_INF)
            sink = sinks_ref[h]
            m = jnp.maximum(jnp.max(s, axis=1, keepdims=True), sink)
            e = jnp.exp(s - m)
            denom = jnp.sum(e, axis=1, keepdims=True) + jnp.exp(sink - m)
            out = _dot((e / denom).astype(BF16), vs)
            o_ref[0, sub * w:(sub + 1) * w, g * HEAD_DIM:(g + 1) * HEAD_DIM] = out.astype(BF16)


def _swa_attention(qkv, sinks, slopes):
    q0 = HEAD_SLOT["swa_q"] // SWA_GROUP
    k0, v0 = HEAD_SLOT["swa_k"], HEAD_SLOT["swa_v"]
    smem = pl.BlockSpec(memory_space=pltpu.SMEM)
    return pl.pallas_call(
        _swa_kernel,
        grid=(BATCH, SWA_KV_HEADS, SEQ // SWA_TQ),
        in_specs=[
            smem, smem,
            pl.BlockSpec((1, SWA_GROUP, SWA_TQ, HEAD_DIM), lambda b, hk, i: (b, q0 + hk, i, 0)),
            pl.BlockSpec((1, 1, SEQ, HEAD_DIM), lambda b, hk, i: (b, k0 + hk, 0, 0)),
            pl.BlockSpec((1, 1, SEQ, HEAD_DIM), lambda b, hk, i: (b, v0 + hk, 0, 0)),
        ],
        out_specs=pl.BlockSpec((1, SWA_TQ, SWA_GROUP * HEAD_DIM), lambda b, hk, i: (b, i, hk)),
        out_shape=jax.ShapeDtypeStruct((BATCH, SEQ, WIDTH), BF16),
        compiler_params=pltpu.CompilerParams(
            dimension_semantics=("arbitrary", "arbitrary", "arbitrary"), vmem_limit_bytes=VMEM_LIMIT),
        name="swa_attention",
    )(sinks, slopes, qkv, qkv, qkv)


def _moba_kernel(slopes_ref, q_ref, k_ref, v_ref, o_ref, kmean_ref, selb_ref):
    hp = pl.program_id(1)
    i = pl.program_id(2)
    t = MOBA_BLOCK
    nb = N_MOBA_BLOCKS

    @pl.when(i == 0)
    def _():
        for hh in range(2):
            def block_mean(bk, _):
                start = pl.multiple_of(bk * t, t)
                kb = k_ref[0, hh, pl.ds(start, t), :].astype(F32)
                kmean_ref[hh, pl.ds(bk, 1), :] = jnp.sum(kb, axis=0, keepdims=True) * (1.0 / t)
                return 0
            lax.fori_loop(0, nb, block_mean, 0)

    blk = lax.broadcasted_iota(jnp.int32, (t, nb), 1)
    row = lax.broadcasted_iota(jnp.int32, (t, t), 0)
    col = lax.broadcasted_iota(jnp.int32, (t, t), 1)
    rel = row - col
    rel_f = rel.astype(F32)
    for hh in range(2):
        slope = slopes_ref[hp * 2 + hh]
        q_raw = q_ref[0, hh]
        q = q_raw * jnp.asarray(SCALE, BF16)

        gate = _dot_nt(q_raw, kmean_ref[hh].astype(BF16))
        gate = jnp.where(blk < i, gate, NEG_INF)
        sel = jnp.zeros((t, nb), jnp.bool_)
        for _ in range(MOBA_TOPK):
            mx = jnp.max(gate, axis=1, keepdims=True)
            first = jnp.min(jnp.where(gate == mx, blk, nb), axis=1, keepdims=True)
            pick = blk == first
            sel = sel | pick
            gate = jnp.where(pick, -jnp.inf, gate)
        selb_ref[...] = jnp.where(sel & (blk < i), 0.0, NEG_INF)
        slope_rel = slope * rel_f

        start = pl.multiple_of(i * t, t)
        s = _dot_nt(q, k_ref[0, hh, pl.ds(start, t), :]) - slope_rel
        s = jnp.where(rel >= 0, s, NEG_INF)
        carry = _online_softmax_step(s, v_ref[0, hh, pl.ds(start, t), :], _softmax_init(t))

        def past_block(j, carry):
            start = pl.multiple_of(j * t, t)
            ks = k_ref[0, hh, pl.ds(start, t), :]
            vs = v_ref[0, hh, pl.ds(start, t), :]
            picked = jnp.sum(jnp.where(blk == j, selb_ref[...], 0.0), axis=1, keepdims=True)
            bias = picked - slope * ((i - j) * t).astype(F32)
            s = _dot_nt(q, ks) + bias - slope_rel
            return _online_softmax_step(s, vs, carry)

        _, l, acc = lax.fori_loop(0, i, past_block, carry)
        o_ref[0, :, hh * HEAD_DIM:(hh + 1) * HEAD_DIM] = (acc / l).astype(BF16)


def _moba_attention(qkv, slopes):
    t = MOBA_BLOCK
    q0, k0, v0 = (HEAD_SLOT[n] // 2 for n in ("moba_q", "moba_k", "moba_v"))
    return pl.pallas_call(
        _moba_kernel,
        grid=(BATCH, N_HEADS // 2, N_MOBA_BLOCKS),
        in_specs=[
            pl.BlockSpec(memory_space=pltpu.SMEM),
            pl.BlockSpec((1, 2, t, HEAD_DIM), lambda b, hp, i: (b, q0 + hp, i, 0)),
            pl.BlockSpec((1, 2, SEQ, HEAD_DIM), lambda b, hp, i: (b, k0 + hp, 0, 0)),
            pl.BlockSpec((1, 2, SEQ, HEAD_DIM), lambda b, hp, i: (b, v0 + hp, 0, 0)),
        ],
        out_specs=pl.BlockSpec((1, t, 2 * HEAD_DIM), lambda b, hp, i: (b, i, hp)),
        out_shape=jax.ShapeDtypeStruct((BATCH, SEQ, WIDTH), BF16),
        scratch_shapes=[pltpu.VMEM((2, N_MOBA_BLOCKS, HEAD_DIM), F32),
                        pltpu.VMEM((t, N_MOBA_BLOCKS), F32)],
        compiler_params=pltpu.CompilerParams(
            dimension_semantics=("arbitrary", "arbitrary", "arbitrary"), vmem_limit_bytes=VMEM_LIMIT),
        name="moba_attention",
    )(slopes, qkv, qkv, qkv)


def _merge_kernel(x_ref, zg_ref, ofox_ref, oswa_ref, omoba_ref, wbr_ref, wout_ref, g_ref, b_ref, o_ref):
    y = jnp.zeros((MERGE_TM, D_MODEL), F32)
    for br, o_br in enumerate((ofox_ref, oswa_ref, omoba_ref)):
        silu_z = zg_ref[:, br * WIDTH:(br + 1) * WIDTH].astype(F32)
        a = (o_br[...].astype(F32) * silu_z).astype(BF16)
        gate = zg_ref[:, 3 * WIDTH + br * D_MODEL:3 * WIDTH + (br + 1) * D_MODEL].astype(F32)
        y = y + gate * _dot(a, wbr_ref[br])
    out = _dot(y.astype(BF16), wout_ref[...])
    r = DEEPNORM_ALPHA * x_ref[...] + out
    mu = jnp.mean(r, axis=1, keepdims=True)
    d = r - mu
    var = jnp.mean(d * d, axis=1, keepdims=True)
    o_ref[...] = d * lax.rsqrt(var + LN_EPS) * g_ref[...] + b_ref[...]


def _merge(x2, zg, o_fox, o_swa, o_moba, w_br, w_out, ln_g, ln_b):
    row_tile = lambda n: pl.BlockSpec((MERGE_TM, n), lambda i: (i, 0))
    return pl.pallas_call(
        _merge_kernel,
        grid=(ROWS // MERGE_TM,),
        in_specs=[
            row_tile(D_MODEL), row_tile(N_ZG), row_tile(WIDTH), row_tile(WIDTH), row_tile(WIDTH),
            pl.BlockSpec((3, WIDTH, D_MODEL), lambda i: (0, 0, 0)),
            pl.BlockSpec((D_MODEL, D_MODEL), lambda i: (0, 0)),
            pl.BlockSpec((1, D_MODEL), lambda i: (0, 0)),
            pl.BlockSpec((1, D_MODEL), lambda i: (0, 0)),
        ],
        out_specs=row_tile(D_MODEL),
        out_shape=jax.ShapeDtypeStruct((ROWS, D_MODEL), F32),
        compiler_params=pltpu.CompilerParams(
            dimension_semantics=("arbitrary",), vmem_limit_bytes=VMEM_LIMIT),
        name="merge_deepnorm",
    )(x2, zg, o_fox, o_swa, o_moba, w_br, w_out, ln_g, ln_b)


def _columns(w, names):
    return jnp.concatenate([w[..., _OFF[n][0]:_OFF[n][1]] for n in names], axis=-1)


def _alibi_slopes(n):
    return jnp.power(2.0, -8.0 * jnp.arange(1, n + 1, dtype=F32) / n)


def _layer(x2, w_in, b_in, sinks, w_br, w_out, ln_g, ln_b):
    zg_names = ("fox_z", "swa_z", "moba_z", "gate_fox", "gate_swa", "gate_moba")
    w_qkv = _columns(w_in, QKV_ORDER).astype(BF16)
    b_qkv = _columns(b_in, QKV_ORDER)[None, :]
    w_f = jnp.pad(_columns(w_in, ("fox_f",)), ((0, 0), (0, F_PAD - N_HEADS))).astype(BF16)
    b_f = jnp.pad(_columns(b_in, ("fox_f",)), (0, F_PAD - N_HEADS))[None, :]
    w_zg = _columns(w_in, zg_names).astype(BF16)
    b_zg = _columns(b_in, zg_names)[None, :]
    slopes = _alibi_slopes(N_HEADS)

    qkv, f_raw = _qkv_proj(x2, w_qkv, b_qkv, w_f, b_f)
    zg = _zg_proj(x2, w_zg, b_zg)
    ccol, crow = _forget_cumsum(f_raw)
    o_fox = _fox_attention(qkv, ccol, crow).reshape(ROWS, WIDTH)
    o_swa = _swa_attention(qkv, sinks, slopes).reshape(ROWS, WIDTH)
    o_moba = _moba_attention(qkv, slopes).reshape(ROWS, WIDTH)
    return _merge(x2, zg, o_fox, o_swa, o_moba, w_br.astype(BF16), w_out.astype(BF16),
                  ln_g[None, :], ln_b[None, :])


def kernel(x, w_in, b_in, swa_sinks, w_branch_fox, w_branch_swa, w_branch_moba, w_out, ln_gain, ln_bias):
    x2 = x.reshape(ROWS, D_MODEL)
    for l in range(DEPTH):
        w_br = jnp.stack([w_branch_fox[l], w_branch_swa[l], w_branch_moba[l]])
        x2 = _layer(x2, w_in[l], b_in[l], swa_sinks[l], w_br, w_out[l], ln_gain[l], ln_bias[l])
    return x2.reshape(BATCH, SEQ, D_MODEL)
```

```python
import math

import jax
import jax.numpy as jnp
from jax import lax
from jax.experimental import pallas as pl
from jax.experimental.pallas import tpu as pltpu

D_MODEL = 2048
BATCH = 2
SEQ = 16384
DEPTH = 2
HEAD_DIM = 64
LANES = 128
N_HEADS = 8
SWA_KV_HEADS = 2
SWA_GROUP = N_HEADS // SWA_KV_HEADS
WIDTH = N_HEADS * HEAD_DIM
SWA_WINDOW = 128
MOBA_BLOCK = 256
MOBA_TOPK = 3
DEEPNORM_ALPHA = (2.0 * DEPTH) ** 0.25
LN_EPS = 1e-5
NEG_INF = -1e30
SCALE = HEAD_DIM ** -0.5
LOG2E = math.log2(math.e)

_SEG = (("fox_q", WIDTH), ("fox_k", WIDTH), ("fox_v", WIDTH), ("fox_z", WIDTH), ("fox_f", N_HEADS),
        ("swa_q", WIDTH), ("swa_k", SWA_KV_HEADS * HEAD_DIM), ("swa_v", SWA_KV_HEADS * HEAD_DIM), ("swa_z", WIDTH),
        ("moba_q", WIDTH), ("moba_k", WIDTH), ("moba_v", WIDTH), ("moba_z", WIDTH),
        ("gate_fox", D_MODEL), ("gate_swa", D_MODEL), ("gate_moba", D_MODEL))
_OFF = {}
_start = 0
for _name, _size in _SEG:
    _OFF[_name] = (_start, _start + _size)
    _start += _size

QKV_ORDER = ("fox_q", "fox_k", "fox_v", "swa_q", "swa_k", "swa_v", "moba_q", "moba_k", "moba_v")
HEAD_SLOT = {}
_slot = 0
for _name in QKV_ORDER:
    HEAD_SLOT[_name] = _slot
    _slot += (_OFF[_name][1] - _OFF[_name][0]) // HEAD_DIM
N_QKV_HEADS = _slot
N_QKV = N_QKV_HEADS * HEAD_DIM
N_ZG = 3 * WIDTH + 3 * D_MODEL
F_PAD = 128

ROWS = BATCH * SEQ
PROJ_TM = 1024
QKV_TN = 768
ZG_TN = 768
ATT_T = 512
ATT_GROUP = 1024
N_ATT_TILES = SEQ // ATT_T
SWA_TQ = 512
MERGE_TM = 256
VMEM_LIMIT = 56 * 1024 * 1024

BIAS_SPLIT = 3
ONES_LANE = HEAD_DIM

F32 = jnp.float32
BF16 = jnp.bfloat16


def _dot(a, b):
    return jnp.dot(a, b, preferred_element_type=F32)


def _dot_nt(a, b):
    return lax.dot_general(a, b, (((1,), (1,)), ((), ())), preferred_element_type=F32)


def _sigmoid(v):
    return 1.0 / (1.0 + jnp.exp(-v))


def _split3(v):
    hi = v.astype(BF16).astype(F32)
    rem = v - hi
    mid = rem.astype(BF16).astype(F32)
    lo = (rem - mid).astype(BF16).astype(F32)
    return hi, mid, lo


def _qkv_proj_kernel(x_ref, w_ref, b_ref, cs_ref, padc_ref, rc1_ref, rc2_ref, wf_ref, bf_ref,
                     o_ref, f_ref, xb_ref):
    j = pl.program_id(1)

    @pl.when(j == 0)
    def _():
        xb_ref[...] = x_ref[...].astype(BF16)
        f_ref[...] = _dot(xb_ref[...], wf_ref[...]) + bf_ref[...]

    acc = (_dot(xb_ref[...], w_ref[...]) + b_ref[...]) * cs_ref[...]
    r = lax.broadcasted_iota(jnp.int32, (PROJ_TM, 1), 0)
    row1 = ((r % MOBA_BLOCK) - (MOBA_BLOCK - 1)).astype(F32)
    row2 = (((r // MOBA_BLOCK) % (ATT_GROUP // MOBA_BLOCK)) * MOBA_BLOCK - (ATT_GROUP - MOBA_BLOCK)).astype(F32)
    lane = lax.broadcasted_iota(jnp.int32, (PROJ_TM, LANES), 1)
    data_lane = lane < HEAD_DIM
    for pair in range(QKV_TN // LANES):
        both = acc[:, pair * LANES:(pair + 1) * LANES]
        swapped = pltpu.roll(both, HEAD_DIM, axis=1)
        for hh, data in ((2 * pair, both), (2 * pair + 1, swapped)):
            cols = slice(hh * LANES, (hh + 1) * LANES)
            bias = padc_ref[:, cols] + rc1_ref[:, cols] * row1 + rc2_ref[:, cols] * row2
            o_ref[0, hh] = jnp.where(data_lane, data, bias).astype(BF16)


def _qkv_proj(x2, w, b, cs, padc, rc1, rc2, wf, bf):
    tiles_per_batch = SEQ // PROJ_TM
    heads_per_step = QKV_TN // HEAD_DIM
    col = lambda n: pl.BlockSpec((1, n), lambda i, j: (0, j))
    return pl.pallas_call(
        _qkv_proj_kernel,
        grid=(ROWS // PROJ_TM, N_QKV // QKV_TN),
        in_specs=[
            pl.BlockSpec((PROJ_TM, D_MODEL), lambda i, j: (i, 0)),
            pl.BlockSpec((D_MODEL, QKV_TN), lambda i, j: (0, j)),
            col(QKV_TN), col(QKV_TN),
            col(heads_per_step * LANES), col(heads_per_step * LANES), col(heads_per_step * LANES),
            pl.BlockSpec((D_MODEL, F_PAD), lambda i, j: (0, 0)),
            pl.BlockSpec((1, F_PAD), lambda i, j: (0, 0)),
        ],
        out_specs=[
            pl.BlockSpec((1, heads_per_step, PROJ_TM, LANES),
                         lambda i, j: (i // tiles_per_batch, j, i % tiles_per_batch, 0)),
            pl.BlockSpec((PROJ_TM, F_PAD), lambda i, j: (i, 0)),
        ],
        out_shape=[
            jax.ShapeDtypeStruct((BATCH, N_QKV_HEADS, SEQ, LANES), BF16),
            jax.ShapeDtypeStruct((ROWS, F_PAD), F32),
        ],
        scratch_shapes=[pltpu.VMEM((PROJ_TM, D_MODEL), BF16)],
        compiler_params=pltpu.CompilerParams(
            dimension_semantics=("arbitrary", "arbitrary"), vmem_limit_bytes=VMEM_LIMIT),
        name="qkv_proj",
    )(x2, w, b, cs, padc, rc1, rc2, wf, bf)


def _zg_proj_kernel(x_ref, w_ref, b_ref, o_ref, xb_ref):
    j = pl.program_id(1)

    @pl.when(j == 0)
    def _():
        xb_ref[...] = x_ref[...].astype(BF16)

    acc = _dot(xb_ref[...], w_ref[...]) + b_ref[...]
    sig = _sigmoid(acc)

    @pl.when(j < (3 * WIDTH) // ZG_TN)
    def _():
        o_ref[...] = (acc * sig).astype(BF16)

    @pl.when(j >= (3 * WIDTH) // ZG_TN)
    def _():
        o_ref[...] = sig.astype(BF16)


def _zg_proj(x2, w, b):
    return pl.pallas_call(
        _zg_proj_kernel,
        grid=(ROWS // PROJ_TM, N_ZG // ZG_TN),
        in_specs=[
            pl.BlockSpec((PROJ_TM, D_MODEL), lambda i, j: (i, 0)),
            pl.BlockSpec((D_MODEL, ZG_TN), lambda i, j: (0, j)),
            pl.BlockSpec((1, ZG_TN), lambda i, j: (0, j)),
        ],
        out_specs=pl.BlockSpec((PROJ_TM, ZG_TN), lambda i, j: (i, j)),
        out_shape=jax.ShapeDtypeStruct((ROWS, N_ZG), BF16),
        scratch_shapes=[pltpu.VMEM((PROJ_TM, D_MODEL), BF16)],
        compiler_params=pltpu.CompilerParams(
            dimension_semantics=("arbitrary", "arbitrary"), vmem_limit_bytes=VMEM_LIMIT),
        name="zg_proj",
    )(x2, w, b)


def _fox_pack_kernel(f_ref, k_ref, place_ref, o_ref, carry_ref):
    t = pl.program_id(1)

    @pl.when(t == 0)
    def _():
        carry_ref[...] = jnp.zeros_like(carry_ref)

    f = f_ref[...]
    log_f = jnp.minimum(f, 0.0) - jnp.log(1.0 + jnp.exp(-jnp.abs(f)))
    row = lax.broadcasted_iota(jnp.int32, (ATT_T, ATT_T), 0)
    col = lax.broadcasted_iota(jnp.int32, (ATT_T, ATT_T), 1)
    tri = (row >= col).astype(BF16)
    pieces = jnp.concatenate([p.astype(BF16) for p in _split3(log_f)], axis=1)
    sums = _dot(tri, pieces)
    cum = (sums[:, 0:F_PAD] + sums[:, F_PAD:2 * F_PAD] + sums[:, 2 * F_PAD:3 * F_PAD]) + carry_ref[...]
    carry_ref[...] = cum[ATT_T - 1:ATT_T, :]
    c_pieces = jnp.concatenate([p.astype(BF16) for p in _split3(cum * LOG2E)], axis=1)
    placed = _dot(c_pieces, place_ref[...])
    for h in range(N_HEADS):
        o_ref[0, h] = (k_ref[0, h].astype(F32) + placed[:, h * LANES:(h + 1) * LANES]).astype(BF16)


def _fox_pack(f_raw, qkv, place):
    k_block = HEAD_SLOT["fox_k"] // N_HEADS
    return pl.pallas_call(
        _fox_pack_kernel,
        grid=(BATCH, N_ATT_TILES),
        in_specs=[
            pl.BlockSpec((ATT_T, F_PAD), lambda b, t: (b * N_ATT_TILES + t, 0)),
            pl.BlockSpec((1, N_HEADS, ATT_T, LANES), lambda b, t: (b, k_block, t, 0)),
            pl.BlockSpec((BIAS_SPLIT * F_PAD, N_HEADS * LANES), lambda b, t: (0, 0)),
        ],
        out_specs=pl.BlockSpec((1, N_HEADS, ATT_T, LANES), lambda b, t: (b, 0, t, 0)),
        out_shape=jax.ShapeDtypeStruct((BATCH, N_HEADS, SEQ, LANES), BF16),
        scratch_shapes=[pltpu.VMEM((1, F_PAD), F32)],
        compiler_params=pltpu.CompilerParams(dimension_semantics=("arbitrary", "arbitrary")),
        name="fox_pack",
    )(f_raw, qkv, place)


def _flash_two_heads(q_aug, k_ref, v_ref, i, key_extra=None, group_offset=None):
    t = ATT_T

    def step(start, width, group, carry, causal):
        offs = None if group_offset is None else group_offset(group)
        extra = None if key_extra is None else key_extra(start, width)
        new = []
        for hh in range(2):
            m, acc = carry[hh]
            ks = k_ref[0, hh, pl.ds(start, width), :]
            if extra is not None:
                ks = jnp.concatenate([ks, extra], axis=1)
            s = _dot_nt(q_aug[hh], ks)
            if causal:
                row = lax.broadcasted_iota(jnp.int32, (t, width), 0)
                col = lax.broadcasted_iota(jnp.int32, (t, width), 1)
                s = jnp.where(col <= row, s, NEG_INF)
            tile_max = jnp.max(s, axis=1, keepdims=True)
            if offs is not None:
                tile_max = tile_max + offs[hh]
            m_new = jnp.maximum(m, tile_max)
            shift = m_new if offs is None else m_new - offs[hh]
            alpha = jnp.exp2(m - m_new)
            p = jnp.exp2(s - shift).astype(BF16)
            acc = alpha * acc + _dot(p, v_ref[0, hh, pl.ds(start, width), :])
            new.append((m_new, acc))
        return tuple(new)

    own_group = i // (ATT_GROUP // t)
    carry = tuple((jnp.full((t, 1), NEG_INF, F32), jnp.zeros((t, LANES), F32)) for _ in range(2))
    carry = lax.fori_loop(
        0, own_group,
        lambda g, c: step(pl.multiple_of(g * ATT_GROUP, ATT_GROUP), ATT_GROUP, g, c, False), carry)
    carry = lax.fori_loop(
        own_group * (ATT_GROUP // t), i,
        lambda j, c: step(pl.multiple_of(j * t, t), t, own_group, c, False), carry)
    carry = step(pl.multiple_of(i * t, t), t, own_group, carry, True)
    return [acc / acc[:, ONES_LANE:ONES_LANE + 1] for _, acc in carry]


def _store_head_pair(o_ref, outs):
    lane = lax.broadcasted_iota(jnp.int32, (ATT_T, LANES), 1)
    o_ref[0] = jnp.where(lane < HEAD_DIM, outs[0], pltpu.roll(outs[1], HEAD_DIM, axis=1)).astype(BF16)


def _fox_kernel(q_ref, k_ref, v_ref, o_ref):
    i = pl.program_id(2)
    outs = _flash_two_heads([q_ref[0, 0], q_ref[0, 1]], k_ref, v_ref, i)
    _store_head_pair(o_ref, outs)


def _fox_attention(qkv, k_aug):
    q0, v0 = HEAD_SLOT["fox_q"] // 2, HEAD_SLOT["fox_v"] // 2
    return pl.pallas_call(
        _fox_kernel,
        grid=(BATCH, N_HEADS // 2, N_ATT_TILES),
        in_specs=[
            pl.BlockSpec((1, 2, ATT_T, LANES), lambda b, hp, i: (b, q0 + hp, i, 0)),
            pl.BlockSpec((1, 2, SEQ, LANES), lambda b, hp, i: (b, hp, 0, 0)),
            pl.BlockSpec((1, 2, SEQ, LANES), lambda b, hp, i: (b, v0 + hp, 0, 0)),
        ],
        out_specs=pl.BlockSpec((1, ATT_T, 2 * HEAD_DIM), lambda b, hp, i: (b, i, hp)),
        out_shape=jax.ShapeDtypeStruct((BATCH, SEQ, WIDTH), BF16),
        compiler_params=pltpu.CompilerParams(
            dimension_semantics=("arbitrary", "arbitrary", "arbitrary"), vmem_limit_bytes=VMEM_LIMIT),
        name="fox_attention",
    )(qkv, k_aug, qkv)


def _swa_kernel(sinks_ref, slopes_ref, q_ref, k_ref, v_ref, o_ref):
    hkv = pl.program_id(1)
    i = pl.program_id(2)
    w = SWA_WINDOW
    qi = lax.broadcasted_iota(jnp.int32, (w, 2 * w), 0)
    ki = lax.broadcasted_iota(jnp.int32, (w, 2 * w), 1)
    for sub in range(SWA_TQ // w):
        q_start = i * SWA_TQ + sub * w
        k_start = pl.multiple_of(jnp.maximum(q_start - w, 0), w)
        ks = k_ref[0, 0, pl.ds(k_start, 2 * w), :]
        vs = v_ref[0, 0, pl.ds(k_start, 2 * w), :]
        rel = (q_start - k_start) + qi - ki
        valid = (rel >= 0) & (rel < w)
        rel_f = rel.astype(F32)
        for g in range(SWA_GROUP):
            h = hkv * SWA_GROUP + g
            q = q_ref[0, g, sub * w:(sub + 1) * w, :] * jnp.asarray(SCALE, BF16)
            s = _dot_nt(q, ks) - slopes_ref[h] * rel_f
            s = jnp.where(valid, s, NEG_INF)
            sink = sinks_ref[h]
            m = jnp.maximum(jnp.max(s, axis=1, keepdims=True), sink)
            e = jnp.exp(s - m)
            denom = jnp.sum(e, axis=1, keepdims=True) + jnp.exp(sink - m)
            out = _dot((e / denom).astype(BF16), vs)[:, 0:HEAD_DIM]
            o_ref[0, sub * w:(sub + 1) * w, g * HEAD_DIM:(g + 1) * HEAD_DIM] = out.astype(BF16)


def _swa_attention(qkv, sinks, slopes):
    q0 = HEAD_SLOT["swa_q"] // SWA_GROUP
    k0, v0 = HEAD_SLOT["swa_k"], HEAD_SLOT["swa_v"]
    smem = pl.BlockSpec(memory_space=pltpu.SMEM)
    return pl.pallas_call(
        _swa_kernel,
        grid=(BATCH, SWA_KV_HEADS, SEQ // SWA_TQ),
        in_specs=[
            smem, smem,
            pl.BlockSpec((1, SWA_GROUP, SWA_TQ, LANES), lambda b, hk, i: (b, q0 + hk, i, 0)),
            pl.BlockSpec((1, 1, SEQ, LANES), lambda b, hk, i: (b, k0 + hk, 0, 0)),
            pl.BlockSpec((1, 1, SEQ, LANES), lambda b, hk, i: (b, v0 + hk, 0, 0)),
        ],
        out_specs=pl.BlockSpec((1, SWA_TQ, SWA_GROUP * HEAD_DIM), lambda b, hk, i: (b, i, hk)),
        out_shape=jax.ShapeDtypeStruct((BATCH, SEQ, WIDTH), BF16),
        compiler_params=pltpu.CompilerParams(
            dimension_semantics=("arbitrary", "arbitrary", "arbitrary"), vmem_limit_bytes=VMEM_LIMIT),
        name="swa_attention",
    )(sinks, slopes, qkv, qkv, qkv)


def _moba_kernel(slope2_ref, q_ref, k_ref, v_ref, o_ref, kmean_ref, onehot_ref):
    hp = pl.program_id(1)
    i = pl.program_id(2)
    t = ATT_T
    blocks_per_tile = t // MOBA_BLOCK
    row_block = lax.broadcasted_iota(jnp.int32, (t, LANES), 0) // MOBA_BLOCK
    blk = lax.broadcasted_iota(jnp.int32, (t, LANES), 1)

    @pl.when(i == 0)
    def _():
        kmean_ref[...] = jnp.zeros_like(kmean_ref)
        lane = lax.broadcasted_iota(jnp.int32, (1, LANES), 1)
        for hh in range(2):
            def block_mean(bk, _):
                start = pl.multiple_of(bk * MOBA_BLOCK, MOBA_BLOCK)
                kb = k_ref[0, hh, pl.ds(start, MOBA_BLOCK), :].astype(F32)
                mean = jnp.sum(kb, axis=0, keepdims=True) * (1.0 / MOBA_BLOCK)
                kmean_ref[hh, pl.ds(bk, 1), :] = jnp.where(lane < HEAD_DIM, mean, 0.0)
                return 0
            lax.fori_loop(0, SEQ // MOBA_BLOCK, block_mean, 0)

        def indicator(j, _):
            rows = pl.ds(pl.multiple_of(j * t, t), t)
            onehot_ref[rows, :] = (blk == blocks_per_tile * j + row_block).astype(BF16)
            return 0
        lax.fori_loop(0, N_ATT_TILES, indicator, 0)

    own = blocks_per_tile * i + row_block
    q_aug = []
    for hh in range(2):
        q = q_ref[0, hh]
        gate = _dot_nt(q, kmean_ref[hh].astype(BF16))
        gate = jnp.where(blk < own, gate, NEG_INF)
        sel = blk == own
        for _ in range(MOBA_TOPK):
            mx = jnp.max(gate, axis=1, keepdims=True)
            first = jnp.min(jnp.where(gate == mx, blk, LANES), axis=1, keepdims=True)
            pick = blk == first
            sel = sel | (pick & (blk < own))
            gate = jnp.where(pick, -jnp.inf, gate)
        select_bias = jnp.where(sel, 0.0, NEG_INF).astype(BF16)
        q_aug.append(jnp.concatenate([q, select_bias], axis=1))

    def group_offset(g):
        keys_back = ((i // (ATT_GROUP // t) - g) * ATT_GROUP).astype(F32)
        return [-(slope2_ref[2 * hp + hh] * keys_back) for hh in range(2)]

    outs = _flash_two_heads(q_aug, k_ref, v_ref, i, group_offset=group_offset,
                            key_extra=lambda start, width: onehot_ref[pl.ds(start, width), :])
    _store_head_pair(o_ref, outs)


def _moba_attention(qkv, slope2):
    q0, k0, v0 = (HEAD_SLOT[n] // 2 for n in ("moba_q", "moba_k", "moba_v"))
    return pl.pallas_call(
        _moba_kernel,
        grid=(BATCH, N_HEADS // 2, N_ATT_TILES),
        in_specs=[
            pl.BlockSpec(memory_space=pltpu.SMEM),
            pl.BlockSpec((1, 2, ATT_T, LANES), lambda b, hp, i: (b, q0 + hp, i, 0)),
            pl.BlockSpec((1, 2, SEQ, LANES), lambda b, hp, i: (b, k0 + hp, 0, 0)),
            pl.BlockSpec((1, 2, SEQ, LANES), lambda b, hp, i: (b, v0 + hp, 0, 0)),
        ],
        out_specs=pl.BlockSpec((1, ATT_T, 2 * HEAD_DIM), lambda b, hp, i: (b, i, hp)),
        out_shape=jax.ShapeDtypeStruct((BATCH, SEQ, WIDTH), BF16),
        scratch_shapes=[pltpu.VMEM((2, LANES, LANES), F32),
                        pltpu.VMEM((SEQ, LANES), BF16)],
        compiler_params=pltpu.CompilerParams(
            dimension_semantics=("arbitrary", "arbitrary", "arbitrary"), vmem_limit_bytes=VMEM_LIMIT),
        name="moba_attention",
    )(slope2, qkv, qkv, qkv)


def _merge_kernel(x_ref, zg_ref, ofox_ref, oswa_ref, omoba_ref, wbr_ref, wout_ref, g_ref, b_ref, o_ref):
    y = jnp.zeros((MERGE_TM, D_MODEL), F32)
    for br, o_br in enumerate((ofox_ref, oswa_ref, omoba_ref)):
        silu_z = zg_ref[:, br * WIDTH:(br + 1) * WIDTH].astype(F32)
        a = (o_br[...].astype(F32) * silu_z).astype(BF16)
        gate = zg_ref[:, 3 * WIDTH + br * D_MODEL:3 * WIDTH + (br + 1) * D_MODEL].astype(F32)
        y = y + gate * _dot(a, wbr_ref[br])
    out = _dot(y.astype(BF16), wout_ref[...])
    r = DEEPNORM_ALPHA * x_ref[...] + out
    mu = jnp.mean(r, axis=1, keepdims=True)
    d = r - mu
    var = jnp.mean(d * d, axis=1, keepdims=True)
    o_ref[...] = d * lax.rsqrt(var + LN_EPS) * g_ref[...] + b_ref[...]


def _merge(x2, zg, o_fox, o_swa, o_moba, w_br, w_out, ln_g, ln_b):
    row_tile = lambda n: pl.BlockSpec((MERGE_TM, n), lambda i: (i, 0))
    return pl.pallas_call(
        _merge_kernel,
        grid=(ROWS // MERGE_TM,),
        in_specs=[
            row_tile(D_MODEL), row_tile(N_ZG), row_tile(WIDTH), row_tile(WIDTH), row_tile(WIDTH),
            pl.BlockSpec((3, WIDTH, D_MODEL), lambda i: (0, 0, 0)),
            pl.BlockSpec((D_MODEL, D_MODEL), lambda i: (0, 0)),
            pl.BlockSpec((1, D_MODEL), lambda i: (0, 0)),
            pl.BlockSpec((1, D_MODEL), lambda i: (0, 0)),
        ],
        out_specs=row_tile(D_MODEL),
        out_shape=jax.ShapeDtypeStruct((ROWS, D_MODEL), F32),
        compiler_params=pltpu.CompilerParams(
            dimension_semantics=("arbitrary",), vmem_limit_bytes=VMEM_LIMIT),
        name="merge_deepnorm",
    )(x2, zg, o_fox, o_swa, o_moba, w_br, w_out, ln_g, ln_b)


def _columns(w, names):
    return jnp.concatenate([w[..., _OFF[n][0]:_OFF[n][1]] for n in names], axis=-1)


def _alibi_slopes(n):
    return jnp.power(2.0, -8.0 * jnp.arange(1, n + 1, dtype=F32) / n)


def _bias_lane_tables():
    cs = jnp.ones((N_QKV_HEADS, HEAD_DIM), F32)
    padc = jnp.zeros((N_QKV_HEADS, LANES), F32)
    rc1 = jnp.zeros((N_QKV_HEADS, LANES), F32)
    rc2 = jnp.zeros((N_QKV_HEADS, LANES), F32)
    heads = lambda name: slice(HEAD_SLOT[name], HEAD_SLOT[name] + N_HEADS)
    b0 = HEAD_DIM
    s3 = BIAS_SPLIT
    cs = cs.at[heads("fox_q")].set(SCALE * LOG2E).at[heads("moba_q")].set(SCALE * LOG2E)
    padc = padc.at[heads("fox_q"), b0:b0 + s3].set(1.0)
    padc = padc.at[heads("fox_v"), ONES_LANE].set(1.0).at[heads("moba_v"), ONES_LANE].set(1.0)
    slope_pieces = jnp.stack(_split3(_alibi_slopes(N_HEADS) * LOG2E), axis=1)
    padc = padc.at[heads("moba_q"), b0:b0 + s3].set(slope_pieces)
    padc = padc.at[heads("moba_q"), b0 + s3:b0 + 2 * s3].set(slope_pieces)
    rc1 = rc1.at[heads("moba_k"), b0:b0 + s3].set(1.0)
    rc2 = rc2.at[heads("moba_k"), b0 + s3:b0 + 2 * s3].set(1.0)
    flat = lambda a: a.reshape(1, -1)
    return flat(cs), flat(padc), flat(rc1), flat(rc2)


def _fox_place_matrix():
    place = jnp.zeros((BIAS_SPLIT * F_PAD, N_HEADS * LANES), F32)
    for p in range(BIAS_SPLIT):
        for h in range(N_HEADS):
            place = place.at[p * F_PAD + h, h * LANES + HEAD_DIM + p].set(-1.0)
    return place.astype(BF16)


def _layer(x2, w_in, b_in, sinks, w_br, w_out, ln_g, ln_b, tables, place):
    zg_names = ("fox_z", "swa_z", "moba_z", "gate_fox", "gate_swa", "gate_moba")
    w_qkv = _columns(w_in, QKV_ORDER).astype(BF16)
    b_qkv = _columns(b_in, QKV_ORDER)[None, :]
    w_f = jnp.pad(_columns(w_in, ("fox_f",)), ((0, 0), (0, F_PAD - N_HEADS))).astype(BF16)
    b_f = jnp.pad(_columns(b_in, ("fox_f",)), (0, F_PAD - N_HEADS))[None, :]
    w_zg = _columns(w_in, zg_names).astype(BF16)
    b_zg = _columns(b_in, zg_names)[None, :]

    qkv, f_raw = _qkv_proj(x2, w_qkv, b_qkv, *tables, w_f, b_f)
    zg = _zg_proj(x2, w_zg, b_zg)
    k_fox = _fox_pack(f_raw, qkv, place)
    o_fox = _fox_attention(qkv, k_fox).reshape(ROWS, WIDTH)
    o_swa = _swa_attention(qkv, sinks, _alibi_slopes(N_HEADS)).reshape(ROWS, WIDTH)
    o_moba = _moba_attention(qkv, _alibi_slopes(N_HEADS) * LOG2E).reshape(ROWS, WIDTH)
    return _merge(x2, zg, o_fox, o_swa, o_moba, w_br.astype(BF16), w_out.astype(BF16),
                  ln_g[None, :], ln_b[None, :])


def kernel(x, w_in, b_in, swa_sinks, w_branch_fox, w_branch_swa, w_branch_moba, w_out, ln_gain, ln_bias):
    x2 = x.reshape(ROWS, D_MODEL)
    tables = _bias_lane_tables()
    place = _fox_place_matrix()
    for l in range(DEPTH):
        w_br = jnp.stack([w_branch_fox[l], w_branch_swa[l], w_branch_moba[l]])
        x2 = _layer(x2, w_in[l], b_in[l], swa_sinks[l], w_br, w_out[l], ln_gain[l], ln_bias[l], tables, place)
    return x2.reshape(BATCH, SEQ, D_MODEL)
```

```python
import math

import jax
import jax.numpy as jnp
from jax import lax
from jax.experimental import pallas as pl
from jax.experimental.pallas import tpu as pltpu

D_MODEL = 2048
BATCH = 2
SEQ = 16384
DEPTH = 2
HEAD_DIM = 64
LANES = 128
N_HEADS = 8
SWA_KV_HEADS = 2
SWA_GROUP = N_HEADS // SWA_KV_HEADS
WIDTH = N_HEADS * HEAD_DIM
SWA_WINDOW = 128
MOBA_BLOCK = 256
MOBA_TOPK = 3
DEEPNORM_ALPHA = (2.0 * DEPTH) ** 0.25
LN_EPS = 1e-5
NEG_INF = -1e30
SCALE = HEAD_DIM ** -0.5
LOG2E = math.log2(math.e)

_SEG = (("fox_q", WIDTH), ("fox_k", WIDTH), ("fox_v", WIDTH), ("fox_z", WIDTH), ("fox_f", N_HEADS),
        ("swa_q", WIDTH), ("swa_k", SWA_KV_HEADS * HEAD_DIM), ("swa_v", SWA_KV_HEADS * HEAD_DIM), ("swa_z", WIDTH),
        ("moba_q", WIDTH), ("moba_k", WIDTH), ("moba_v", WIDTH), ("moba_z", WIDTH),
        ("gate_fox", D_MODEL), ("gate_swa", D_MODEL), ("gate_moba", D_MODEL))
_OFF = {}
_start = 0
for _name, _size in _SEG:
    _OFF[_name] = (_start, _start + _size)
    _start += _size

QKV_ORDER = ("fox_q", "fox_k", "fox_v", "swa_q", "swa_k", "swa_v", "moba_q", "moba_k", "moba_v")
HEAD_SLOT = {}
_slot = 0
for _name in QKV_ORDER:
    HEAD_SLOT[_name] = _slot
    _slot += (_OFF[_name][1] - _OFF[_name][0]) // HEAD_DIM
N_QKV_HEADS = _slot
N_QKV = N_QKV_HEADS * HEAD_DIM
N_ZG = 3 * WIDTH + 3 * D_MODEL
F_PAD = 128

ROWS = BATCH * SEQ
PROJ_TM = 1024
QKV_TN = 1280
ZG_TN = 1536
ATT_T = 512
ATT_GROUP = 1024
ATT_HEADS = 4
N_ATT_TILES = SEQ // ATT_T
SWA_TQ = 512
MERGE_TM = 256
VMEM_LIMIT = 56 * 1024 * 1024

BIAS_SPLIT = 3
ONES_LANE = HEAD_DIM

F32 = jnp.float32
BF16 = jnp.bfloat16


def _dot(a, b):
    return jnp.dot(a, b, preferred_element_type=F32)


def _dot_nt(a, b):
    return lax.dot_general(a, b, (((1,), (1,)), ((), ())), preferred_element_type=F32)


def _sigmoid(v):
    return 1.0 / (1.0 + jnp.exp(-v))


def _split3(v):
    hi = v.astype(BF16).astype(F32)
    rem = v - hi
    mid = rem.astype(BF16).astype(F32)
    lo = (rem - mid).astype(BF16).astype(F32)
    return hi, mid, lo


def _qkv_proj_kernel(x_ref, w_ref, b_ref, cs_ref, padc_ref, rc1_ref, rc2_ref, wf_ref, bf_ref,
                     o_ref, f_ref, xb_ref):
    j = pl.program_id(1)

    @pl.when(j == 0)
    def _():
        xb_ref[...] = x_ref[...].astype(BF16)
        f_ref[...] = _dot(xb_ref[...], wf_ref[...]) + bf_ref[...]

    acc = (_dot(xb_ref[...], w_ref[...]) + b_ref[...]) * cs_ref[...]
    r = lax.broadcasted_iota(jnp.int32, (PROJ_TM, 1), 0)
    row1 = ((r % MOBA_BLOCK) - (MOBA_BLOCK - 1)).astype(F32)
    row2 = (((r // MOBA_BLOCK) % (ATT_GROUP // MOBA_BLOCK)) * MOBA_BLOCK - (ATT_GROUP - MOBA_BLOCK)).astype(F32)
    lane = lax.broadcasted_iota(jnp.int32, (PROJ_TM, LANES), 1)
    data_lane = lane < HEAD_DIM
    for pair in range(QKV_TN // LANES):
        both = acc[:, pair * LANES:(pair + 1) * LANES]
        swapped = pltpu.roll(both, HEAD_DIM, axis=1)
        for hh, data in ((2 * pair, both), (2 * pair + 1, swapped)):
            cols = slice(hh * LANES, (hh + 1) * LANES)
            bias = padc_ref[:, cols] + rc1_ref[:, cols] * row1 + rc2_ref[:, cols] * row2
            o_ref[0, hh] = jnp.where(data_lane, data, bias).astype(BF16)


def _qkv_proj(x2, w, b, cs, padc, rc1, rc2, wf, bf):
    tiles_per_batch = SEQ // PROJ_TM
    heads_per_step = QKV_TN // HEAD_DIM
    col = lambda n: pl.BlockSpec((1, n), lambda i, j: (0, j))
    return pl.pallas_call(
        _qkv_proj_kernel,
        grid=(ROWS // PROJ_TM, N_QKV // QKV_TN),
        in_specs=[
            pl.BlockSpec((PROJ_TM, D_MODEL), lambda i, j: (i, 0)),
            pl.BlockSpec((D_MODEL, QKV_TN), lambda i, j: (0, j)),
            col(QKV_TN), col(QKV_TN),
            col(heads_per_step * LANES), col(heads_per_step * LANES), col(heads_per_step * LANES),
            pl.BlockSpec((D_MODEL, F_PAD), lambda i, j: (0, 0)),
            pl.BlockSpec((1, F_PAD), lambda i, j: (0, 0)),
        ],
        out_specs=[
            pl.BlockSpec((1, heads_per_step, PROJ_TM, LANES),
                         lambda i, j: (i // tiles_per_batch, j, i % tiles_per_batch, 0)),
            pl.BlockSpec((PROJ_TM, F_PAD), lambda i, j: (i, 0)),
        ],
        out_shape=[
            jax.ShapeDtypeStruct((BATCH, N_QKV_HEADS, SEQ, LANES), BF16),
            jax.ShapeDtypeStruct((ROWS, F_PAD), F32),
        ],
        scratch_shapes=[pltpu.VMEM((PROJ_TM, D_MODEL), BF16)],
        compiler_params=pltpu.CompilerParams(
            dimension_semantics=("arbitrary", "arbitrary"), vmem_limit_bytes=VMEM_LIMIT),
        name="qkv_proj",
    )(x2, w, b, cs, padc, rc1, rc2, wf, bf)


def _zg_proj_kernel(x_ref, w_ref, b_ref, o_ref, xb_ref):
    j = pl.program_id(1)

    @pl.when(j == 0)
    def _():
        xb_ref[...] = x_ref[...].astype(BF16)

    acc = _dot(xb_ref[...], w_ref[...]) + b_ref[...]
    sig = _sigmoid(acc)

    @pl.when(j < (3 * WIDTH) // ZG_TN)
    def _():
        o_ref[...] = (acc * sig).astype(BF16)

    @pl.when(j >= (3 * WIDTH) // ZG_TN)
    def _():
        o_ref[...] = sig.astype(BF16)


def _zg_proj(x2, w, b):
    return pl.pallas_call(
        _zg_proj_kernel,
        grid=(ROWS // PROJ_TM, N_ZG // ZG_TN),
        in_specs=[
            pl.BlockSpec((PROJ_TM, D_MODEL), lambda i, j: (i, 0)),
            pl.BlockSpec((D_MODEL, ZG_TN), lambda i, j: (0, j)),
            pl.BlockSpec((1, ZG_TN), lambda i, j: (0, j)),
        ],
        out_specs=pl.BlockSpec((PROJ_TM, ZG_TN), lambda i, j: (i, j)),
        out_shape=jax.ShapeDtypeStruct((ROWS, N_ZG), BF16),
        scratch_shapes=[pltpu.VMEM((PROJ_TM, D_MODEL), BF16)],
        compiler_params=pltpu.CompilerParams(
            dimension_semantics=("arbitrary", "arbitrary"), vmem_limit_bytes=VMEM_LIMIT),
        name="zg_proj",
    )(x2, w, b)


def _fox_pack_kernel(f_ref, k_ref, place_ref, o_ref, carry_ref):
    t = pl.program_id(1)

    @pl.when(t == 0)
    def _():
        carry_ref[...] = jnp.zeros_like(carry_ref)

    f = f_ref[...]
    log_f = jnp.minimum(f, 0.0) - jnp.log(1.0 + jnp.exp(-jnp.abs(f)))
    row = lax.broadcasted_iota(jnp.int32, (ATT_T, ATT_T), 0)
    col = lax.broadcasted_iota(jnp.int32, (ATT_T, ATT_T), 1)
    tri = (row >= col).astype(BF16)
    pieces = jnp.concatenate([p.astype(BF16) for p in _split3(log_f)], axis=1)
    sums = _dot(tri, pieces)
    cum = (sums[:, 0:F_PAD] + sums[:, F_PAD:2 * F_PAD] + sums[:, 2 * F_PAD:3 * F_PAD]) + carry_ref[...]
    carry_ref[...] = cum[ATT_T - 1:ATT_T, :]
    c_pieces = jnp.concatenate([p.astype(BF16) for p in _split3(cum * LOG2E)], axis=1)
    placed = _dot(c_pieces, place_ref[...])
    for h in range(N_HEADS):
        o_ref[0, h] = (k_ref[0, h].astype(F32) + placed[:, h * LANES:(h + 1) * LANES]).astype(BF16)


def _fox_pack(f_raw, qkv, place):
    k_block = HEAD_SLOT["fox_k"] // N_HEADS
    return pl.pallas_call(
        _fox_pack_kernel,
        grid=(BATCH, N_ATT_TILES),
        in_specs=[
            pl.BlockSpec((ATT_T, F_PAD), lambda b, t: (b * N_ATT_TILES + t, 0)),
            pl.BlockSpec((1, N_HEADS, ATT_T, LANES), lambda b, t: (b, k_block, t, 0)),
            pl.BlockSpec((BIAS_SPLIT * F_PAD, N_HEADS * LANES), lambda b, t: (0, 0)),
        ],
        out_specs=pl.BlockSpec((1, N_HEADS, ATT_T, LANES), lambda b, t: (b, 0, t, 0)),
        out_shape=jax.ShapeDtypeStruct((BATCH, N_HEADS, SEQ, LANES), BF16),
        scratch_shapes=[pltpu.VMEM((1, F_PAD), F32)],
        compiler_params=pltpu.CompilerParams(dimension_semantics=("arbitrary", "arbitrary")),
        name="fox_pack",
    )(f_raw, qkv, place)


def _flash_heads(q_aug, k_ref, v_ref, i, key_extra=None, group_offset=None):
    t = ATT_T

    def step(start, width, group, carry, causal):
        offs = None if group_offset is None else group_offset(group)
        extra = None if key_extra is None else key_extra(start, width)
        scores = []
        for hh in range(ATT_HEADS):
            ks = k_ref[0, hh, pl.ds(start, width), :]
            if extra is not None:
                ks = jnp.concatenate([ks, extra], axis=1)
            scores.append(_dot_nt(q_aug[hh], ks))
        new = []
        for hh in range(ATT_HEADS):
            m, acc = carry[hh]
            s = scores[hh]
            if causal:
                row = lax.broadcasted_iota(jnp.int32, (t, width), 0)
                col = lax.broadcasted_iota(jnp.int32, (t, width), 1)
                s = jnp.where(col <= row, s, NEG_INF)
            tile_max = jnp.max(s, axis=1, keepdims=True)
            if offs is not None:
                tile_max = tile_max + offs[hh]
            m_new = jnp.maximum(m, tile_max)
            shift = m_new if offs is None else m_new - offs[hh]
            alpha = jnp.exp2(m - m_new)
            p = jnp.exp2(s - shift).astype(BF16)
            acc = alpha * acc + _dot(p, v_ref[0, hh, pl.ds(start, width), :])
            new.append((m_new, acc))
        return tuple(new)

    own_group = i // (ATT_GROUP // t)
    carry = tuple((jnp.full((t, 1), NEG_INF, F32), jnp.zeros((t, LANES), F32)) for _ in range(ATT_HEADS))
    carry = lax.fori_loop(
        0, own_group,
        lambda g, c: step(pl.multiple_of(g * ATT_GROUP, ATT_GROUP), ATT_GROUP, g, c, False), carry)
    carry = lax.fori_loop(
        own_group * (ATT_GROUP // t), i,
        lambda j, c: step(pl.multiple_of(j * t, t), t, own_group, c, False), carry)
    carry = step(pl.multiple_of(i * t, t), t, own_group, carry, True)
    return [acc / acc[:, ONES_LANE:ONES_LANE + 1] for _, acc in carry]


def _store_heads(o_ref, outs):
    lane = lax.broadcasted_iota(jnp.int32, (ATT_T, LANES), 1)
    for pair in range(ATT_HEADS // 2):
        both = jnp.where(lane < HEAD_DIM, outs[2 * pair], pltpu.roll(outs[2 * pair + 1], HEAD_DIM, axis=1))
        o_ref[0, :, pair * LANES:(pair + 1) * LANES] = both.astype(BF16)


def _attention_specs(q_slot, k_index, v_slot):
    q0, v0 = q_slot // ATT_HEADS, v_slot // ATT_HEADS
    resident = lambda index: pl.BlockSpec((1, ATT_HEADS, SEQ, LANES), index, pipeline_mode=pl.Buffered(1))
    in_specs = [
        pl.BlockSpec((1, ATT_HEADS, ATT_T, LANES), lambda b, hg, i: (b, q0 + hg, i, 0)),
        resident(lambda b, hg, i: (b, k_index + hg, 0, 0)),
        resident(lambda b, hg, i: (b, v0 + hg, 0, 0)),
    ]
    out_spec = pl.BlockSpec((1, ATT_T, ATT_HEADS * HEAD_DIM), lambda b, hg, i: (b, i, hg))
    return in_specs, out_spec


def _fox_kernel(q_ref, k_ref, v_ref, o_ref):
    i = pl.program_id(2)
    outs = _flash_heads([q_ref[0, hh] for hh in range(ATT_HEADS)], k_ref, v_ref, i)
    _store_heads(o_ref, outs)


def _fox_attention(qkv, k_aug):
    in_specs, out_spec = _attention_specs(HEAD_SLOT["fox_q"], 0, HEAD_SLOT["fox_v"])
    return pl.pallas_call(
        _fox_kernel,
        grid=(BATCH, N_HEADS // ATT_HEADS, N_ATT_TILES),
        in_specs=in_specs,
        out_specs=out_spec,
        out_shape=jax.ShapeDtypeStruct((BATCH, SEQ, WIDTH), BF16),
        compiler_params=pltpu.CompilerParams(
            dimension_semantics=("arbitrary", "arbitrary", "arbitrary"), vmem_limit_bytes=VMEM_LIMIT),
        name="fox_attention",
    )(qkv, k_aug, qkv)


def _swa_kernel(sinks_ref, slopes_ref, q_ref, k_ref, v_ref, o_ref):
    hkv = pl.program_id(1)
    i = pl.program_id(2)
    w = SWA_WINDOW
    qi = lax.broadcasted_iota(jnp.int32, (w, 2 * w), 0)
    ki = lax.broadcasted_iota(jnp.int32, (w, 2 * w), 1)
    for sub in range(SWA_TQ // w):
        q_start = i * SWA_TQ + sub * w
        k_start = pl.multiple_of(jnp.maximum(q_start - w, 0), w)
        ks = k_ref[0, 0, pl.ds(k_start, 2 * w), :]
        vs = v_ref[0, 0, pl.ds(k_start, 2 * w), :]
        rel = (q_start - k_start) + qi - ki
        valid = (rel >= 0) & (rel < w)
        rel_f = rel.astype(F32)
        for g in range(SWA_GROUP):
            h = hkv * SWA_GROUP + g
            q = q_ref[0, g, sub * w:(sub + 1) * w, :] * jnp.asarray(SCALE, BF16)
            s = _dot_nt(q, ks) - slopes_ref[h] * rel_f
            s = jnp.where(valid, s, NEG_INF)
            sink = sinks_ref[h]
            m = jnp.maximum(jnp.max(s, axis=1, keepdims=True), sink)
            e = jnp.exp(s - m)
            denom = jnp.sum(e, axis=1, keepdims=True) + jnp.exp(sink - m)
            out = _dot((e / denom).astype(BF16), vs)[:, 0:HEAD_DIM]
            o_ref[0, sub * w:(sub + 1) * w, g * HEAD_DIM:(g + 1) * HEAD_DIM] = out.astype(BF16)


def _swa_attention(qkv, sinks, slopes):
    q0 = HEAD_SLOT["swa_q"] // SWA_GROUP
    k0, v0 = HEAD_SLOT["swa_k"], HEAD_SLOT["swa_v"]
    smem = pl.BlockSpec(memory_space=pltpu.SMEM)
    return pl.pallas_call(
        _swa_kernel,
        grid=(BATCH, SWA_KV_HEADS, SEQ // SWA_TQ),
        in_specs=[
            smem, smem,
            pl.BlockSpec((1, SWA_GROUP, SWA_TQ, LANES), lambda b, hk, i: (b, q0 + hk, i, 0)),
            pl.BlockSpec((1, 1, SEQ, LANES), lambda b, hk, i: (b, k0 + hk, 0, 0)),
            pl.BlockSpec((1, 1, SEQ, LANES), lambda b, hk, i: (b, v0 + hk, 0, 0)),
        ],
        out_specs=pl.BlockSpec((1, SWA_TQ, SWA_GROUP * HEAD_DIM), lambda b, hk, i: (b, i, hk)),
        out_shape=jax.ShapeDtypeStruct((BATCH, SEQ, WIDTH), BF16),
        compiler_params=pltpu.CompilerParams(
            dimension_semantics=("arbitrary", "arbitrary", "arbitrary"), vmem_limit_bytes=VMEM_LIMIT),
        name="swa_attention",
    )(sinks, slopes, qkv, qkv, qkv)


def _moba_kernel(slope2_ref, q_ref, k_ref, v_ref, o_ref, kmean_ref, onehot_ref):
    hg = pl.program_id(1)
    i = pl.program_id(2)
    t = ATT_T
    blocks_per_tile = t // MOBA_BLOCK
    row_block = lax.broadcasted_iota(jnp.int32, (t, LANES), 0) // MOBA_BLOCK
    blk = lax.broadcasted_iota(jnp.int32, (t, LANES), 1)

    @pl.when(i == 0)
    def _():
        kmean_ref[...] = jnp.zeros_like(kmean_ref)
        lane = lax.broadcasted_iota(jnp.int32, (1, LANES), 1)
        for hh in range(ATT_HEADS):
            def block_mean(bk, _):
                start = pl.multiple_of(bk * MOBA_BLOCK, MOBA_BLOCK)
                kb = k_ref[0, hh, pl.ds(start, MOBA_BLOCK), :].astype(F32)
                mean = jnp.sum(kb, axis=0, keepdims=True) * (1.0 / MOBA_BLOCK)
                kmean_ref[hh, pl.ds(bk, 1), :] = jnp.where(lane < HEAD_DIM, mean, 0.0)
                return 0
            lax.fori_loop(0, SEQ // MOBA_BLOCK, block_mean, 0)

        def indicator(j, _):
            rows = pl.ds(pl.multiple_of(j * t, t), t)
            onehot_ref[rows, :] = (blk == blocks_per_tile * j + row_block).astype(BF16)
            return 0
        lax.fori_loop(0, N_ATT_TILES, indicator, 0)

    blk_t = lax.broadcasted_iota(jnp.int32, (LANES, t), 0)
    own_t = blocks_per_tile * i + lax.broadcasted_iota(jnp.int32, (LANES, t), 1) // MOBA_BLOCK
    past = blk_t < own_t
    blk_f = blk_t.astype(F32)
    q_aug = []
    for hh in range(ATT_HEADS):
        q = q_ref[0, hh]
        gate = _dot_nt(kmean_ref[hh].astype(BF16), q)
        gate = jnp.where(past, gate, NEG_INF)
        sel = blk_t == own_t
        for _ in range(MOBA_TOPK):
            mx = jnp.max(gate, axis=0, keepdims=True)
            first = jnp.min(jnp.where(gate == mx, blk_f, float(LANES)), axis=0, keepdims=True)
            pick = blk_f == first
            sel = sel | (pick & past)
            gate = jnp.where(pick, -jnp.inf, gate)
        select_bias = jnp.where(sel, 0.0, NEG_INF).T.astype(BF16)
        q_aug.append(jnp.concatenate([q, select_bias], axis=1))

    def group_offset(g):
        keys_back = ((i // (ATT_GROUP // t) - g) * ATT_GROUP).astype(F32)
        return [-(slope2_ref[ATT_HEADS * hg + hh] * keys_back) for hh in range(ATT_HEADS)]

    outs = _flash_heads(q_aug, k_ref, v_ref, i, group_offset=group_offset,
                        key_extra=lambda start, width: onehot_ref[pl.ds(start, width), :])
    _store_heads(o_ref, outs)


def _moba_attention(qkv, slope2):
    in_specs, out_spec = _attention_specs(
        HEAD_SLOT["moba_q"], HEAD_SLOT["moba_k"] // ATT_HEADS, HEAD_SLOT["moba_v"])
    return pl.pallas_call(
        _moba_kernel,
        grid=(BATCH, N_HEADS // ATT_HEADS, N_ATT_TILES),
        in_specs=[pl.BlockSpec(memory_space=pltpu.SMEM)] + in_specs,
        out_specs=out_spec,
        out_shape=jax.ShapeDtypeStruct((BATCH, SEQ, WIDTH), BF16),
        scratch_shapes=[pltpu.VMEM((ATT_HEADS, LANES, LANES), F32),
                        pltpu.VMEM((SEQ, LANES), BF16)],
        compiler_params=pltpu.CompilerParams(
            dimension_semantics=("arbitrary", "arbitrary", "arbitrary"), vmem_limit_bytes=VMEM_LIMIT),
        name="moba_attention",
    )(slope2, qkv, qkv, qkv)


def _merge_kernel(x_ref, zg_ref, ofox_ref, oswa_ref, omoba_ref, wbr_ref, wout_ref, g_ref, b_ref, o_ref):
    y = jnp.zeros((MERGE_TM, D_MODEL), F32)
    for br, o_br in enumerate((ofox_ref, oswa_ref, omoba_ref)):
        silu_z = zg_ref[:, br * WIDTH:(br + 1) * WIDTH].astype(F32)
        a = (o_br[...].astype(F32) * silu_z).astype(BF16)
        gate = zg_ref[:, 3 * WIDTH + br * D_MODEL:3 * WIDTH + (br + 1) * D_MODEL].astype(F32)
        y = y + gate * _dot(a, wbr_ref[br])
    out = _dot(y.astype(BF16), wout_ref[...])
    r = DEEPNORM_ALPHA * x_ref[...] + out
    mu = jnp.mean(r, axis=1, keepdims=True)
    d = r - mu
    var = jnp.mean(d * d, axis=1, keepdims=True)
    o_ref[...] = d * lax.rsqrt(var + LN_EPS) * g_ref[...] + b_ref[...]


def _merge(x2, zg, o_fox, o_swa, o_moba, w_br, w_out, ln_g, ln_b):
    row_tile = lambda n: pl.BlockSpec((MERGE_TM, n), lambda i: (i, 0))
    return pl.pallas_call(
        _merge_kernel,
        grid=(ROWS // MERGE_TM,),
        in_specs=[
            row_tile(D_MODEL), row_tile(N_ZG), row_tile(WIDTH), row_tile(WIDTH), row_tile(WIDTH),
            pl.BlockSpec((3, WIDTH, D_MODEL), lambda i: (0, 0, 0)),
            pl.BlockSpec((D_MODEL, D_MODEL), lambda i: (0, 0)),
            pl.BlockSpec((1, D_MODEL), lambda i: (0, 0)),
            pl.BlockSpec((1, D_MODEL), lambda i: (0, 0)),
        ],
        out_specs=row_tile(D_MODEL),
        out_shape=jax.ShapeDtypeStruct((ROWS, D_MODEL), F32),
        compiler_params=pltpu.CompilerParams(
            dimension_semantics=("arbitrary",), vmem_limit_bytes=VMEM_LIMIT),
        name="merge_deepnorm",
    )(x2, zg, o_fox, o_swa, o_moba, w_br, w_out, ln_g, ln_b)


def _columns(w, names):
    return jnp.concatenate([w[..., _OFF[n][0]:_OFF[n][1]] for n in names], axis=-1)


def _alibi_slopes(n):
    return jnp.power(2.0, -8.0 * jnp.arange(1, n + 1, dtype=F32) / n)


def _bias_lane_tables():
    cs = jnp.ones((N_QKV_HEADS, HEAD_DIM), F32)
    padc = jnp.zeros((N_QKV_HEADS, LANES), F32)
    rc1 = jnp.zeros((N_QKV_HEADS, LANES), F32)
    rc2 = jnp.zeros((N_QKV_HEADS, LANES), F32)
    heads = lambda name: slice(HEAD_SLOT[name], HEAD_SLOT[name] + N_HEADS)
    b0 = HEAD_DIM
    s3 = BIAS_SPLIT
    cs = cs.at[heads("fox_q")].set(SCALE * LOG2E).at[heads("moba_q")].set(SCALE * LOG2E)
    padc = padc.at[heads("fox_q"), b0:b0 + s3].set(1.0)
    padc = padc.at[heads("fox_v"), ONES_LANE].set(1.0).at[heads("moba_v"), ONES_LANE].set(1.0)
    slope_pieces = jnp.stack(_split3(_alibi_slopes(N_HEADS) * LOG2E), axis=1)
    padc = padc.at[heads("moba_q"), b0:b0 + s3].set(slope_pieces)
    padc = padc.at[heads("moba_q"), b0 + s3:b0 + 2 * s3].set(slope_pieces)
    rc1 = rc1.at[heads("moba_k"), b0:b0 + s3].set(1.0)
    rc2 = rc2.at[heads("moba_k"), b0 + s3:b0 + 2 * s3].set(1.0)
    flat = lambda a: a.reshape(1, -1)
    return flat(cs), flat(padc), flat(rc1), flat(rc2)


def _fox_place_matrix():
    place = jnp.zeros((BIAS_SPLIT * F_PAD, N_HEADS * LANES), F32)
    for p in range(BIAS_SPLIT):
        for h in range(N_HEADS):
            place = place.at[p * F_PAD + h, h * LANES + HEAD_DIM + p].set(-1.0)
    return place.astype(BF16)


def _layer(x2, w_in, b_in, sinks, w_br, w_out, ln_g, ln_b, tables, place):
    zg_names = ("fox_z", "swa_z", "moba_z", "gate_fox", "gate_swa", "gate_moba")
    w_qkv = _columns(w_in, QKV_ORDER).astype(BF16)
    b_qkv = _columns(b_in, QKV_ORDER)[None, :]
    w_f = jnp.pad(_columns(w_in, ("fox_f",)), ((0, 0), (0, F_PAD - N_HEADS))).astype(BF16)
    b_f = jnp.pad(_columns(b_in, ("fox_f",)), (0, F_PAD - N_HEADS))[None, :]
    w_zg = _columns(w_in, zg_names).astype(BF16)
    b_zg = _columns(b_in, zg_names)[None, :]

    qkv, f_raw = _qkv_proj(x2, w_qkv, b_qkv, *tables, w_f, b_f)
    zg = _zg_proj(x2, w_zg, b_zg)
    k_fox = _fox_pack(f_raw, qkv, place)
    o_fox = _fox_attention(qkv, k_fox).reshape(ROWS, WIDTH)
    o_swa = _swa_attention(qkv, sinks, _alibi_slopes(N_HEADS)).reshape(ROWS, WIDTH)
    o_moba = _moba_attention(qkv, _alibi_slopes(N_HEADS) * LOG2E).reshape(ROWS, WIDTH)
    return _merge(x2, zg, o_fox, o_swa, o_moba, w_br.astype(BF16), w_out.astype(BF16),
                  ln_g[None, :], ln_b[None, :])


def kernel(x, w_in, b_in, swa_sinks, w_branch_fox, w_branch_swa, w_branch_moba, w_out, ln_gain, ln_bias):
    x2 = x.reshape(ROWS, D_MODEL)
    tables = _bias_lane_tables()
    place = _fox_place_matrix()
    for l in range(DEPTH):
        w_br = jnp.stack([w_branch_fox[l], w_branch_swa[l], w_branch_moba[l]])
        x2 = _layer(x2, w_in[l], b_in[l], swa_sinks[l], w_br, w_out[l], ln_gain[l], ln_bias[l], tables, place)
    return x2.reshape(BATCH, SEQ, D_MODEL)
```

```python
import math

import jax
import jax.numpy as jnp
from jax import lax
from jax.experimental import pallas as pl
from jax.experimental.pallas import tpu as pltpu

D_MODEL = 2048
BATCH = 2
SEQ = 16384
DEPTH = 2
HEAD_DIM = 64
LANES = 128
N_HEADS = 8
SWA_KV_HEADS = 2
SWA_GROUP = N_HEADS // SWA_KV_HEADS
WIDTH = N_HEADS * HEAD_DIM
SWA_WINDOW = 128
MOBA_BLOCK = 256
MOBA_TOPK = 3
DEEPNORM_ALPHA = (2.0 * DEPTH) ** 0.25
LN_EPS = 1e-5
NEG_INF = -1e30
SCALE = HEAD_DIM ** -0.5
LOG2E = math.log2(math.e)

_SEG = (("fox_q", WIDTH), ("fox_k", WIDTH), ("fox_v", WIDTH), ("fox_z", WIDTH), ("fox_f", N_HEADS),
        ("swa_q", WIDTH), ("swa_k", SWA_KV_HEADS * HEAD_DIM), ("swa_v", SWA_KV_HEADS * HEAD_DIM), ("swa_z", WIDTH),
        ("moba_q", WIDTH), ("moba_k", WIDTH), ("moba_v", WIDTH), ("moba_z", WIDTH),
        ("gate_fox", D_MODEL), ("gate_swa", D_MODEL), ("gate_moba", D_MODEL))
_OFF = {}
_start = 0
for _name, _size in _SEG:
    _OFF[_name] = (_start, _start + _size)
    _start += _size

QKV_ORDER = ("fox_q", "fox_k", "fox_v", "swa_q", "swa_k", "swa_v", "moba_q", "moba_k", "moba_v")
HEAD_SLOT = {}
_slot = 0
for _name in QKV_ORDER:
    HEAD_SLOT[_name] = _slot
    _slot += (_OFF[_name][1] - _OFF[_name][0]) // HEAD_DIM
N_QKV_HEADS = _slot
N_QKV = N_QKV_HEADS * HEAD_DIM
N_ZG = 3 * WIDTH + 3 * D_MODEL
F_PAD = 128

ROWS = BATCH * SEQ
PROJ_TM = 1024
QKV_TN = 1280
ZG_TN = 1536
ATT_T = 512
ATT_GROUP = 1024
ATT_HEADS = 4
N_ATT_TILES = SEQ // ATT_T
SWA_TQ = 512
MERGE_TM = 256
VMEM_LIMIT = 56 * 1024 * 1024

BIAS_SPLIT = 3
ONES_LANE = HEAD_DIM
UNDERFLOW_BITS = 152.0
BOUND_SLACK = 8.0

F32 = jnp.float32
BF16 = jnp.bfloat16


def _dot(a, b):
    return jnp.dot(a, b, preferred_element_type=F32)


def _dot_nt(a, b):
    return lax.dot_general(a, b, (((1,), (1,)), ((), ())), preferred_element_type=F32)


def _sigmoid(v):
    return 1.0 / (1.0 + jnp.exp(-v))


def _split3(v):
    hi = v.astype(BF16).astype(F32)
    rem = v - hi
    mid = rem.astype(BF16).astype(F32)
    lo = (rem - mid).astype(BF16).astype(F32)
    return hi, mid, lo


def _qkv_proj_kernel(x_ref, w_ref, b_ref, cs_ref, padc_ref, rc1_ref, rc2_ref, wf_ref, bf_ref,
                     o_ref, f_ref, xb_ref):
    j = pl.program_id(1)

    @pl.when(j == 0)
    def _():
        xb_ref[...] = x_ref[...].astype(BF16)
        f_ref[...] = _dot(xb_ref[...], wf_ref[...]) + bf_ref[...]

    acc = (_dot(xb_ref[...], w_ref[...]) + b_ref[...]) * cs_ref[...]
    r = lax.broadcasted_iota(jnp.int32, (PROJ_TM, 1), 0)
    row1 = ((r % MOBA_BLOCK) - (MOBA_BLOCK - 1)).astype(F32)
    row2 = (((r // MOBA_BLOCK) % (ATT_GROUP // MOBA_BLOCK)) * MOBA_BLOCK - (ATT_GROUP - MOBA_BLOCK)).astype(F32)
    lane = lax.broadcasted_iota(jnp.int32, (PROJ_TM, LANES), 1)
    data_lane = lane < HEAD_DIM
    for pair in range(QKV_TN // LANES):
        both = acc[:, pair * LANES:(pair + 1) * LANES]
        swapped = pltpu.roll(both, HEAD_DIM, axis=1)
        for hh, data in ((2 * pair, both), (2 * pair + 1, swapped)):
            cols = slice(hh * LANES, (hh + 1) * LANES)
            bias = padc_ref[:, cols] + rc1_ref[:, cols] * row1 + rc2_ref[:, cols] * row2
            o_ref[0, hh] = jnp.where(data_lane, data, bias).astype(BF16)


def _qkv_proj(x2, w, b, cs, padc, rc1, rc2, wf, bf):
    tiles_per_batch = SEQ // PROJ_TM
    heads_per_step = QKV_TN // HEAD_DIM
    col = lambda n: pl.BlockSpec((1, n), lambda i, j: (0, j))
    return pl.pallas_call(
        _qkv_proj_kernel,
        grid=(ROWS // PROJ_TM, N_QKV // QKV_TN),
        in_specs=[
            pl.BlockSpec((PROJ_TM, D_MODEL), lambda i, j: (i, 0)),
            pl.BlockSpec((D_MODEL, QKV_TN), lambda i, j: (0, j)),
            col(QKV_TN), col(QKV_TN),
            col(heads_per_step * LANES), col(heads_per_step * LANES), col(heads_per_step * LANES),
            pl.BlockSpec((D_MODEL, F_PAD), lambda i, j: (0, 0)),
            pl.BlockSpec((1, F_PAD), lambda i, j: (0, 0)),
        ],
        out_specs=[
            pl.BlockSpec((1, heads_per_step, PROJ_TM, LANES),
                         lambda i, j: (i // tiles_per_batch, j, i % tiles_per_batch, 0)),
            pl.BlockSpec((PROJ_TM, F_PAD), lambda i, j: (i, 0)),
        ],
        out_shape=[
            jax.ShapeDtypeStruct((BATCH, N_QKV_HEADS, SEQ, LANES), BF16),
            jax.ShapeDtypeStruct((ROWS, F_PAD), F32),
        ],
        scratch_shapes=[pltpu.VMEM((PROJ_TM, D_MODEL), BF16)],
        compiler_params=pltpu.CompilerParams(
            dimension_semantics=("arbitrary", "arbitrary"), vmem_limit_bytes=VMEM_LIMIT),
        name="qkv_proj",
    )(x2, w, b, cs, padc, rc1, rc2, wf, bf)


def _zg_proj_kernel(x_ref, w_ref, b_ref, o_ref, xb_ref):
    j = pl.program_id(1)

    @pl.when(j == 0)
    def _():
        xb_ref[...] = x_ref[...].astype(BF16)

    acc = _dot(xb_ref[...], w_ref[...]) + b_ref[...]
    sig = _sigmoid(acc)

    @pl.when(j < (3 * WIDTH) // ZG_TN)
    def _():
        o_ref[...] = (acc * sig).astype(BF16)

    @pl.when(j >= (3 * WIDTH) // ZG_TN)
    def _():
        o_ref[...] = sig.astype(BF16)


def _zg_proj(x2, w, b):
    return pl.pallas_call(
        _zg_proj_kernel,
        grid=(ROWS // PROJ_TM, N_ZG // ZG_TN),
        in_specs=[
            pl.BlockSpec((PROJ_TM, D_MODEL), lambda i, j: (i, 0)),
            pl.BlockSpec((D_MODEL, ZG_TN), lambda i, j: (0, j)),
            pl.BlockSpec((1, ZG_TN), lambda i, j: (0, j)),
        ],
        out_specs=pl.BlockSpec((PROJ_TM, ZG_TN), lambda i, j: (i, j)),
        out_shape=jax.ShapeDtypeStruct((ROWS, N_ZG), BF16),
        scratch_shapes=[pltpu.VMEM((PROJ_TM, D_MODEL), BF16)],
        compiler_params=pltpu.CompilerParams(
            dimension_semantics=("arbitrary", "arbitrary"), vmem_limit_bytes=VMEM_LIMIT),
        name="zg_proj",
    )(x2, w, b)


def _fox_pack_kernel(f_ref, k_ref, place_ref, o_ref, carry_ref):
    t = pl.program_id(1)

    @pl.when(t == 0)
    def _():
        carry_ref[...] = jnp.zeros_like(carry_ref)

    f = f_ref[...]
    log_f = jnp.minimum(f, 0.0) - jnp.log(1.0 + jnp.exp(-jnp.abs(f)))
    row = lax.broadcasted_iota(jnp.int32, (ATT_T, ATT_T), 0)
    col = lax.broadcasted_iota(jnp.int32, (ATT_T, ATT_T), 1)
    tri = (row >= col).astype(BF16)
    pieces = jnp.concatenate([p.astype(BF16) for p in _split3(log_f)], axis=1)
    sums = _dot(tri, pieces)
    cum = (sums[:, 0:F_PAD] + sums[:, F_PAD:2 * F_PAD] + sums[:, 2 * F_PAD:3 * F_PAD]) + carry_ref[...]
    carry_ref[...] = cum[ATT_T - 1:ATT_T, :]
    c_pieces = jnp.concatenate([p.astype(BF16) for p in _split3(cum * LOG2E)], axis=1)
    placed = _dot(c_pieces, place_ref[...])
    for h in range(N_HEADS):
        o_ref[0, h] = (k_ref[0, h].astype(F32) + placed[:, h * LANES:(h + 1) * LANES]).astype(BF16)


def _fox_pack(f_raw, qkv, place):
    k_block = HEAD_SLOT["fox_k"] // N_HEADS
    return pl.pallas_call(
        _fox_pack_kernel,
        grid=(BATCH, N_ATT_TILES),
        in_specs=[
            pl.BlockSpec((ATT_T, F_PAD), lambda b, t: (b * N_ATT_TILES + t, 0)),
            pl.BlockSpec((1, N_HEADS, ATT_T, LANES), lambda b, t: (b, k_block, t, 0)),
            pl.BlockSpec((BIAS_SPLIT * F_PAD, N_HEADS * LANES), lambda b, t: (0, 0)),
        ],
        out_specs=pl.BlockSpec((1, N_HEADS, ATT_T, LANES), lambda b, t: (b, 0, t, 0)),
        out_shape=jax.ShapeDtypeStruct((BATCH, N_HEADS, SEQ, LANES), BF16),
        scratch_shapes=[pltpu.VMEM((1, F_PAD), F32)],
        compiler_params=pltpu.CompilerParams(dimension_semantics=("arbitrary", "arbitrary")),
        name="fox_pack",
    )(f_raw, qkv, place)


def _row_norm_max(x):
    lane = lax.broadcasted_iota(jnp.int32, (1, LANES), 1)
    xf = jnp.where(lane < HEAD_DIM, x.astype(F32), 0.0)
    return jnp.sqrt(jnp.max(jnp.sum(xf * xf, axis=1, keepdims=True), axis=0, keepdims=True))


def _store_key_norm_max(k_ref, kmax_ref):
    for hh in range(ATT_HEADS):
        def chunk(c, best):
            rows = pl.ds(pl.multiple_of(c * ATT_T, ATT_T), ATT_T)
            return jnp.maximum(best, _row_norm_max(k_ref[0, hh, rows, :]))
        best = lax.fori_loop(0, N_ATT_TILES, chunk, jnp.zeros((1, 1), F32))
        kmax_ref[hh] = jnp.broadcast_to(best, kmax_ref.shape[1:])


def _flash_heads(q_aug, k_ref, v_ref, i, key_extra=None, group_offset=None, older_bound=None):
    t = ATT_T

    def step(start, width, group, carry, causal):
        offs = None if group_offset is None else group_offset(group)
        extra = None if key_extra is None else key_extra(start, width)
        scores = []
        for hh in range(ATT_HEADS):
            ks = k_ref[0, hh, pl.ds(start, width), :]
            if extra is not None:
                ks = jnp.concatenate([ks, extra], axis=1)
            scores.append(_dot_nt(q_aug[hh], ks))
        new = []
        for hh in range(ATT_HEADS):
            m, acc = carry[hh]
            s = scores[hh]
            if causal:
                row = lax.broadcasted_iota(jnp.int32, (t, width), 0)
                col = lax.broadcasted_iota(jnp.int32, (t, width), 1)
                s = jnp.where(col <= row, s, NEG_INF)
            tile_max = jnp.max(s, axis=1, keepdims=True)
            if offs is not None:
                tile_max = tile_max + offs[hh]
            m_new = jnp.maximum(m, tile_max)
            shift = m_new if offs is None else m_new - offs[hh]
            alpha = jnp.exp2(m - m_new)
            p = jnp.exp2(s - shift).astype(BF16)
            acc = alpha * acc + _dot(p, v_ref[0, hh, pl.ds(start, width), :])
            new.append((m_new, acc))
        return tuple(new)

    def still_needed(g, carry):
        if older_bound is None:
            return jnp.int32(1)
        worst = None
        for hh in range(ATT_HEADS):
            gap = older_bound(hh, jnp.maximum(g, 0)) - carry[hh][0]
            worst = gap if worst is None else jnp.maximum(worst, gap)
        return (jnp.max(worst) > -UNDERFLOW_BITS).astype(jnp.int32)

    own_group = i // (ATT_GROUP // t)
    carry = tuple((jnp.full((t, 1), NEG_INF, F32), jnp.zeros((t, LANES), F32)) for _ in range(ATT_HEADS))
    carry = step(pl.multiple_of(i * t, t), t, own_group, carry, True)
    carry = lax.fori_loop(
        own_group * (ATT_GROUP // t), i,
        lambda j, c: step(pl.multiple_of(j * t, t), t, own_group, c, False), carry)

    def older_group(state):
        g, _, c = state
        c = step(pl.multiple_of(g * ATT_GROUP, ATT_GROUP), ATT_GROUP, g, c, False)
        return g - 1, still_needed(g - 1, c), c

    _, _, carry = lax.while_loop(
        lambda state: jnp.logical_and(state[0] >= 0, state[1] == 1), older_group,
        (own_group - 1, still_needed(own_group - 1, carry), carry))
    return [acc / acc[:, ONES_LANE:ONES_LANE + 1] for _, acc in carry]


def _store_heads(o_ref, outs):
    lane = lax.broadcasted_iota(jnp.int32, (ATT_T, LANES), 1)
    for pair in range(ATT_HEADS // 2):
        both = jnp.where(lane < HEAD_DIM, outs[2 * pair], pltpu.roll(outs[2 * pair + 1], HEAD_DIM, axis=1))
        o_ref[0, :, pair * LANES:(pair + 1) * LANES] = both.astype(BF16)


def _attention_specs(q_slot, k_index, v_slot):
    q0, v0 = q_slot // ATT_HEADS, v_slot // ATT_HEADS
    resident = lambda index: pl.BlockSpec((1, ATT_HEADS, SEQ, LANES), index, pipeline_mode=pl.Buffered(1))
    in_specs = [
        pl.BlockSpec((1, ATT_HEADS, ATT_T, LANES), lambda b, hg, i: (b, q0 + hg, i, 0)),
        resident(lambda b, hg, i: (b, k_index + hg, 0, 0)),
        resident(lambda b, hg, i: (b, v0 + hg, 0, 0)),
    ]
    out_spec = pl.BlockSpec((1, ATT_T, ATT_HEADS * HEAD_DIM), lambda b, hg, i: (b, i, hg))
    return in_specs, out_spec


def _fox_kernel(q_ref, k_ref, v_ref, o_ref, kmax_ref):
    i = pl.program_id(2)

    @pl.when(i == 0)
    def _():
        _store_key_norm_max(k_ref, kmax_ref)

    q = [q_ref[0, hh] for hh in range(ATT_HEADS)]
    qk_bound = [_row_norm_max(q[hh]) * kmax_ref[hh][0:1, 0:1] + BOUND_SLACK for hh in range(ATT_HEADS)]
    lane = lax.broadcasted_iota(jnp.int32, (1, LANES), 1)
    bias_lanes = (lane >= HEAD_DIM) & (lane < HEAD_DIM + BIAS_SPLIT)
    tail = 16

    def older_bound(hh, g):
        rows = pl.ds(pl.multiple_of((g + 1) * ATT_GROUP - tail, tail), tail)
        bias = jnp.sum(jnp.where(bias_lanes, k_ref[0, hh, rows, :].astype(F32), 0.0), axis=1, keepdims=True)
        return qk_bound[hh] + jnp.max(bias, axis=0, keepdims=True)

    outs = _flash_heads(q, k_ref, v_ref, i, older_bound=older_bound)
    _store_heads(o_ref, outs)


def _fox_attention(qkv, k_aug):
    in_specs, out_spec = _attention_specs(HEAD_SLOT["fox_q"], 0, HEAD_SLOT["fox_v"])
    return pl.pallas_call(
        _fox_kernel,
        grid=(BATCH, N_HEADS // ATT_HEADS, N_ATT_TILES),
        in_specs=in_specs,
        out_specs=out_spec,
        out_shape=jax.ShapeDtypeStruct((BATCH, SEQ, WIDTH), BF16),
        scratch_shapes=[pltpu.VMEM((ATT_HEADS, 8, LANES), F32)],
        compiler_params=pltpu.CompilerParams(
            dimension_semantics=("arbitrary", "arbitrary", "arbitrary"), vmem_limit_bytes=VMEM_LIMIT),
        name="fox_attention",
    )(qkv, k_aug, qkv)


def _swa_kernel(sinks_ref, slopes_ref, q_ref, k_ref, v_ref, o_ref):
    hkv = pl.program_id(1)
    i = pl.program_id(2)
    w = SWA_WINDOW
    qi = lax.broadcasted_iota(jnp.int32, (w, 2 * w), 0)
    ki = lax.broadcasted_iota(jnp.int32, (w, 2 * w), 1)
    for sub in range(SWA_TQ // w):
        q_start = i * SWA_TQ + sub * w
        k_start = pl.multiple_of(jnp.maximum(q_start - w, 0), w)
        ks = k_ref[0, 0, pl.ds(k_start, 2 * w), :]
        vs = v_ref[0, 0, pl.ds(k_start, 2 * w), :]
        rel = (q_start - k_start) + qi - ki
        valid = (rel >= 0) & (rel < w)
        rel_f = rel.astype(F32)
        for g in range(SWA_GROUP):
            h = hkv * SWA_GROUP + g
            q = q_ref[0, g, sub * w:(sub + 1) * w, :] * jnp.asarray(SCALE, BF16)
            s = _dot_nt(q, ks) - slopes_ref[h] * rel_f
            s = jnp.where(valid, s, NEG_INF)
            sink = sinks_ref[h]
            m = jnp.maximum(jnp.max(s, axis=1, keepdims=True), sink)
            e = jnp.exp(s - m)
            denom = jnp.sum(e, axis=1, keepdims=True) + jnp.exp(sink - m)
            out = _dot((e / denom).astype(BF16), vs)[:, 0:HEAD_DIM]
            o_ref[0, sub * w:(sub + 1) * w, g * HEAD_DIM:(g + 1) * HEAD_DIM] = out.astype(BF16)


def _swa_attention(qkv, sinks, slopes):
    q0 = HEAD_SLOT["swa_q"] // SWA_GROUP
    k0, v0 = HEAD_SLOT["swa_k"], HEAD_SLOT["swa_v"]
    smem = pl.BlockSpec(memory_space=pltpu.SMEM)
    return pl.pallas_call(
        _swa_kernel,
        grid=(BATCH, SWA_KV_HEADS, SEQ // SWA_TQ),
        in_specs=[
            smem, smem,
            pl.BlockSpec((1, SWA_GROUP, SWA_TQ, LANES), lambda b, hk, i: (b, q0 + hk, i, 0)),
            pl.BlockSpec((1, 1, SEQ, LANES), lambda b, hk, i: (b, k0 + hk, 0, 0)),
            pl.BlockSpec((1, 1, SEQ, LANES), lambda b, hk, i: (b, v0 + hk, 0, 0)),
        ],
        out_specs=pl.BlockSpec((1, SWA_TQ, SWA_GROUP * HEAD_DIM), lambda b, hk, i: (b, i, hk)),
        out_shape=jax.ShapeDtypeStruct((BATCH, SEQ, WIDTH), BF16),
        compiler_params=pltpu.CompilerParams(
            dimension_semantics=("arbitrary", "arbitrary", "arbitrary"), vmem_limit_bytes=VMEM_LIMIT),
        name="swa_attention",
    )(sinks, slopes, qkv, qkv, qkv)


def _moba_kernel(slope2_ref, q_ref, k_ref, v_ref, o_ref, kmean_ref, onehot_ref, kmax_ref):
    hg = pl.program_id(1)
    i = pl.program_id(2)
    t = ATT_T
    blocks_per_tile = t // MOBA_BLOCK
    row_block = lax.broadcasted_iota(jnp.int32, (t, LANES), 0) // MOBA_BLOCK
    blk = lax.broadcasted_iota(jnp.int32, (t, LANES), 1)

    @pl.when(i == 0)
    def _():
        _store_key_norm_max(k_ref, kmax_ref)
        kmean_ref[...] = jnp.zeros_like(kmean_ref)
        lane = lax.broadcasted_iota(jnp.int32, (1, LANES), 1)
        for hh in range(ATT_HEADS):
            def block_mean(bk, _):
                start = pl.multiple_of(bk * MOBA_BLOCK, MOBA_BLOCK)
                kb = k_ref[0, hh, pl.ds(start, MOBA_BLOCK), :].astype(F32)
                mean = jnp.sum(kb, axis=0, keepdims=True) * (1.0 / MOBA_BLOCK)
                kmean_ref[hh, pl.ds(bk, 1), :] = jnp.where(lane < HEAD_DIM, mean, 0.0)
                return 0
            lax.fori_loop(0, SEQ // MOBA_BLOCK, block_mean, 0)

        def indicator(j, _):
            rows = pl.ds(pl.multiple_of(j * t, t), t)
            onehot_ref[rows, :] = (blk == blocks_per_tile * j + row_block).astype(BF16)
            return 0
        lax.fori_loop(0, N_ATT_TILES, indicator, 0)

    blk_t = lax.broadcasted_iota(jnp.int32, (LANES, t), 0)
    own_t = blocks_per_tile * i + lax.broadcasted_iota(jnp.int32, (LANES, t), 1) // MOBA_BLOCK
    past = blk_t < own_t
    blk_f = blk_t.astype(F32)
    q_aug = []
    for hh in range(ATT_HEADS):
        q = q_ref[0, hh]
        gate = _dot_nt(kmean_ref[hh].astype(BF16), q)
        gate = jnp.where(past, gate, NEG_INF)
        sel = blk_t == own_t
        for _ in range(MOBA_TOPK):
            mx = jnp.max(gate, axis=0, keepdims=True)
            first = jnp.min(jnp.where(gate == mx, blk_f, float(LANES)), axis=0, keepdims=True)
            pick = blk_f == first
            sel = sel | (pick & past)
            gate = jnp.where(pick, -jnp.inf, gate)
        select_bias = jnp.where(sel, 0.0, NEG_INF).T.astype(BF16)
        q_aug.append(jnp.concatenate([q, select_bias], axis=1))

    def group_offset(g):
        keys_back = ((i // (ATT_GROUP // t) - g) * ATT_GROUP).astype(F32)
        return [-(slope2_ref[ATT_HEADS * hg + hh] * keys_back) for hh in range(ATT_HEADS)]

    qk_bound = [_row_norm_max(q_ref[0, hh]) * kmax_ref[hh][0:1, 0:1] + BOUND_SLACK for hh in range(ATT_HEADS)]

    def older_bound(hh, g):
        return qk_bound[hh] + group_offset(g)[hh]

    outs = _flash_heads(q_aug, k_ref, v_ref, i, group_offset=group_offset, older_bound=older_bound,
                        key_extra=lambda start, width: onehot_ref[pl.ds(start, width), :])
    _store_heads(o_ref, outs)


def _moba_attention(qkv, slope2):
    in_specs, out_spec = _attention_specs(
        HEAD_SLOT["moba_q"], HEAD_SLOT["moba_k"] // ATT_HEADS, HEAD_SLOT["moba_v"])
    return pl.pallas_call(
        _moba_kernel,
        grid=(BATCH, N_HEADS // ATT_HEADS, N_ATT_TILES),
        in_specs=[pl.BlockSpec(memory_space=pltpu.SMEM)] + in_specs,
        out_specs=out_spec,
        out_shape=jax.ShapeDtypeStruct((BATCH, SEQ, WIDTH), BF16),
        scratch_shapes=[pltpu.VMEM((ATT_HEADS, LANES, LANES), F32),
                        pltpu.VMEM((SEQ, LANES), BF16),
                        pltpu.VMEM((ATT_HEADS, 8, LANES), F32)],
        compiler_params=pltpu.CompilerParams(
            dimension_semantics=("arbitrary", "arbitrary", "arbitrary"), vmem_limit_bytes=VMEM_LIMIT),
        name="moba_attention",
    )(slope2, qkv, qkv, qkv)


def _merge_kernel(x_ref, zg_ref, ofox_ref, oswa_ref, omoba_ref, wbr_ref, wout_ref, g_ref, b_ref, o_ref):
    y = jnp.zeros((MERGE_TM, D_MODEL), F32)
    for br, o_br in enumerate((ofox_ref, oswa_ref, omoba_ref)):
        silu_z = zg_ref[:, br * WIDTH:(br + 1) * WIDTH].astype(F32)
        a = (o_br[...].astype(F32) * silu_z).astype(BF16)
        gate = zg_ref[:, 3 * WIDTH + br * D_MODEL:3 * WIDTH + (br + 1) * D_MODEL].astype(F32)
        y = y + gate * _dot(a, wbr_ref[br])
    out = _dot(y.astype(BF16), wout_ref[...])
    r = DEEPNORM_ALPHA * x_ref[...] + out
    mu = jnp.mean(r, axis=1, keepdims=True)
    d = r - mu
    var = jnp.mean(d * d, axis=1, keepdims=True)
    o_ref[...] = d * lax.rsqrt(var + LN_EPS) * g_ref[...] + b_ref[...]


def _merge(x2, zg, o_fox, o_swa, o_moba, w_br, w_out, ln_g, ln_b):
    row_tile = lambda n: pl.BlockSpec((MERGE_TM, n), lambda i: (i, 0))
    return pl.pallas_call(
        _merge_kernel,
        grid=(ROWS // MERGE_TM,),
        in_specs=[
            row_tile(D_MODEL), row_tile(N_ZG), row_tile(WIDTH), row_tile(WIDTH), row_tile(WIDTH),
            pl.BlockSpec((3, WIDTH, D_MODEL), lambda i: (0, 0, 0)),
            pl.BlockSpec((D_MODEL, D_MODEL), lambda i: (0, 0)),
            pl.BlockSpec((1, D_MODEL), lambda i: (0, 0)),
            pl.BlockSpec((1, D_MODEL), lambda i: (0, 0)),
        ],
        out_specs=row_tile(D_MODEL),
        out_shape=jax.ShapeDtypeStruct((ROWS, D_MODEL), F32),
        compiler_params=pltpu.CompilerParams(
            dimension_semantics=("arbitrary",), vmem_limit_bytes=VMEM_LIMIT),
        name="merge_deepnorm",
    )(x2, zg, o_fox, o_swa, o_moba, w_br, w_out, ln_g, ln_b)


def _columns(w, names):
    return jnp.concatenate([w[..., _OFF[n][0]:_OFF[n][1]] for n in names], axis=-1)


def _alibi_slopes(n):
    return jnp.power(2.0, -8.0 * jnp.arange(1, n + 1, dtype=F32) / n)


def _bias_lane_tables():
    cs = jnp.ones((N_QKV_HEADS, HEAD_DIM), F32)
    padc = jnp.zeros((N_QKV_HEADS, LANES), F32)
    rc1 = jnp.zeros((N_QKV_HEADS, LANES), F32)
    rc2 = jnp.zeros((N_QKV_HEADS, LANES), F32)
    heads = lambda name: slice(HEAD_SLOT[name], HEAD_SLOT[name] + N_HEADS)
    b0 = HEAD_DIM
    s3 = BIAS_SPLIT
    cs = cs.at[heads("fox_q")].set(SCALE * LOG2E).at[heads("moba_q")].set(SCALE * LOG2E)
    padc = padc.at[heads("fox_q"), b0:b0 + s3].set(1.0)
    padc = padc.at[heads("fox_v"), ONES_LANE].set(1.0).at[heads("moba_v"), ONES_LANE].set(1.0)
    slope_pieces = jnp.stack(_split3(_alibi_slopes(N_HEADS) * LOG2E), axis=1)
    padc = padc.at[heads("moba_q"), b0:b0 + s3].set(slope_pieces)
    padc = padc.at[heads("moba_q"), b0 + s3:b0 + 2 * s3].set(slope_pieces)
    rc1 = rc1.at[heads("moba_k"), b0:b0 + s3].set(1.0)
    rc2 = rc2.at[heads("moba_k"), b0 + s3:b0 + 2 * s3].set(1.0)
    flat = lambda a: a.reshape(1, -1)
    return flat(cs), flat(padc), flat(rc1), flat(rc2)


def _fox_place_matrix():
    place = jnp.zeros((BIAS_SPLIT * F_PAD, N_HEADS * LANES), F32)
    for p in range(BIAS_SPLIT):
        for h in range(N_HEADS):
            place = place.at[p * F_PAD + h, h * LANES + HEAD_DIM + p].set(-1.0)
    return place.astype(BF16)


def _layer(x2, w_in, b_in, sinks, w_br, w_out, ln_g, ln_b, tables, place):
    zg_names = ("fox_z", "swa_z", "moba_z", "gate_fox", "gate_swa", "gate_moba")
    w_qkv = _columns(w_in, QKV_ORDER).astype(BF16)
    b_qkv = _columns(b_in, QKV_ORDER)[None, :]
    w_f = jnp.pad(_columns(w_in, ("fox_f",)), ((0, 0), (0, F_PAD - N_HEADS))).astype(BF16)
    b_f = jnp.pad(_columns(b_in, ("fox_f",)), (0, F_PAD - N_HEADS))[None, :]
    w_zg = _columns(w_in, zg_names).astype(BF16)
    b_zg = _columns(b_in, zg_names)[None, :]

    qkv, f_raw = _qkv_proj(x2, w_qkv, b_qkv, *tables, w_f, b_f)
    zg = _zg_proj(x2, w_zg, b_zg)
    k_fox = _fox_pack(f_raw, qkv, place)
    o_fox = _fox_attention(qkv, k_fox).reshape(ROWS, WIDTH)
    o_swa = _swa_attention(qkv, sinks, _alibi_slopes(N_HEADS)).reshape(ROWS, WIDTH)
    o_moba = _moba_attention(qkv, _alibi_slopes(N_HEADS) * LOG2E).reshape(ROWS, WIDTH)
    return _merge(x2, zg, o_fox, o_swa, o_moba, w_br.astype(BF16), w_out.astype(BF16),
                  ln_g[None, :], ln_b[None, :])


def kernel(x, w_in, b_in, swa_sinks, w_branch_fox, w_branch_swa, w_branch_moba, w_out, ln_gain, ln_bias):
    x2 = x.reshape(ROWS, D_MODEL)
    tables = _bias_lane_tables()
    place = _fox_place_matrix()
    for l in range(DEPTH):
        w_br = jnp.stack([w_branch_fox[l], w_branch_swa[l], w_branch_moba[l]])
        x2 = _layer(x2, w_in[l], b_in[l], swa_sinks[l], w_br, w_out[l], ln_gain[l], ln_bias[l], tables, place)
    return x2.reshape(BATCH, SEQ, D_MODEL)
```

```python
import math

import jax
import jax.numpy as jnp
from jax import lax
from jax.experimental import pallas as pl
from jax.experimental.pallas import tpu as pltpu

D_MODEL = 2048
BATCH = 2
SEQ = 16384
DEPTH = 2
HEAD_DIM = 64
LANES = 128
N_HEADS = 8
SWA_KV_HEADS = 2
SWA_GROUP = N_HEADS // SWA_KV_HEADS
WIDTH = N_HEADS * HEAD_DIM
SWA_WINDOW = 128
MOBA_BLOCK = 256
MOBA_TOPK = 3
N_MOBA_BLOCKS = SEQ // MOBA_BLOCK
DEEPNORM_ALPHA = (2.0 * DEPTH) ** 0.25
LN_EPS = 1e-5
NEG_INF = -1e30
SCALE = HEAD_DIM ** -0.5
LOG2E = math.log2(math.e)

_SEG = (("fox_q", WIDTH), ("fox_k", WIDTH), ("fox_v", WIDTH), ("fox_z", WIDTH), ("fox_f", N_HEADS),
        ("swa_q", WIDTH), ("swa_k", SWA_KV_HEADS * HEAD_DIM), ("swa_v", SWA_KV_HEADS * HEAD_DIM), ("swa_z", WIDTH),
        ("moba_q", WIDTH), ("moba_k", WIDTH), ("moba_v", WIDTH), ("moba_z", WIDTH),
        ("gate_fox", D_MODEL), ("gate_swa", D_MODEL), ("gate_moba", D_MODEL))
_OFF = {}
_start = 0
for _name, _size in _SEG:
    _OFF[_name] = (_start, _start + _size)
    _start += _size

QKV_ORDER = ("fox_q", "fox_k", "fox_v", "swa_q", "swa_k", "swa_v", "moba_q", "moba_k", "moba_v")
HEAD_SLOT = {}
_slot = 0
for _name in QKV_ORDER:
    HEAD_SLOT[_name] = _slot
    _slot += (_OFF[_name][1] - _OFF[_name][0]) // HEAD_DIM
N_QKV_HEADS = _slot
N_QKV = N_QKV_HEADS * HEAD_DIM
N_ZG = 3 * WIDTH + 3 * D_MODEL
F_PAD = 128

ROWS = BATCH * SEQ
PROJ_TM = 1024
QKV_TN = 1280
ZG_TN = 1536
ATT_T = 512
ATT_GROUP = 1024
ATT_HEADS = 4
N_ATT_TILES = SEQ // ATT_T
ROW_BIAS_HEADS = frozenset(
    (HEAD_SLOT["moba_k"] + h) % (QKV_TN // HEAD_DIM) for h in range(N_HEADS))
SWA_TQ = 512
MERGE_TM = 512
VMEM_LIMIT = 56 * 1024 * 1024

BIAS_SPLIT = 3
ONES_LANE = HEAD_DIM
UNDERFLOW_BITS = 152.0
BOUND_SLACK = 8.0

F32 = jnp.float32
BF16 = jnp.bfloat16


def _dot(a, b):
    return jnp.dot(a, b, preferred_element_type=F32)


def _dot_nt(a, b):
    return lax.dot_general(a, b, (((1,), (1,)), ((), ())), preferred_element_type=F32)


def _sigmoid(v):
    return 1.0 / (1.0 + jnp.exp(-v))


def _split3(v):
    hi = v.astype(BF16).astype(F32)
    rem = v - hi
    mid = rem.astype(BF16).astype(F32)
    lo = (rem - mid).astype(BF16).astype(F32)
    return hi, mid, lo


def _qkv_proj_kernel(x_ref, w_ref, b_ref, cs_ref, padc_ref, rc1_ref, rc2_ref, wf_ref, bf_ref,
                     o_ref, f_ref, xb_ref):
    j = pl.program_id(1)

    @pl.when(j == 0)
    def _():
        xb_ref[...] = x_ref[...].astype(BF16)
        f_ref[...] = _dot(xb_ref[...], wf_ref[...]) + bf_ref[...]

    acc = (_dot(xb_ref[...], w_ref[...]) + b_ref[...]) * cs_ref[...]
    r = lax.broadcasted_iota(jnp.int32, (PROJ_TM, 1), 0)
    row1 = ((r % MOBA_BLOCK) - (MOBA_BLOCK - 1)).astype(F32)
    row2 = (((r // MOBA_BLOCK) % (ATT_GROUP // MOBA_BLOCK)) * MOBA_BLOCK - (ATT_GROUP - MOBA_BLOCK)).astype(F32)
    lane = lax.broadcasted_iota(jnp.int32, (PROJ_TM, LANES), 1)
    data_lane = lane < HEAD_DIM
    for pair in range(QKV_TN // LANES):
        both = acc[:, pair * LANES:(pair + 1) * LANES]
        swapped = pltpu.roll(both, HEAD_DIM, axis=1)
        for hh, data in ((2 * pair, both), (2 * pair + 1, swapped)):
            cols = slice(hh * LANES, (hh + 1) * LANES)
            bias = padc_ref[:, cols]
            if hh in ROW_BIAS_HEADS:
                bias = bias + rc1_ref[:, cols] * row1 + rc2_ref[:, cols] * row2
            o_ref[0, hh] = jnp.where(data_lane, data, bias).astype(BF16)


def _qkv_proj(x2, w, b, cs, padc, rc1, rc2, wf, bf):
    tiles_per_batch = SEQ // PROJ_TM
    heads_per_step = QKV_TN // HEAD_DIM
    col = lambda n: pl.BlockSpec((1, n), lambda i, j: (0, j))
    return pl.pallas_call(
        _qkv_proj_kernel,
        grid=(ROWS // PROJ_TM, N_QKV // QKV_TN),
        in_specs=[
            pl.BlockSpec((PROJ_TM, D_MODEL), lambda i, j: (i, 0)),
            pl.BlockSpec((D_MODEL, QKV_TN), lambda i, j: (0, j)),
            col(QKV_TN), col(QKV_TN),
            col(heads_per_step * LANES), col(heads_per_step * LANES), col(heads_per_step * LANES),
            pl.BlockSpec((D_MODEL, F_PAD), lambda i, j: (0, 0)),
            pl.BlockSpec((1, F_PAD), lambda i, j: (0, 0)),
        ],
        out_specs=[
            pl.BlockSpec((1, heads_per_step, PROJ_TM, LANES),
                         lambda i, j: (i // tiles_per_batch, j, i % tiles_per_batch, 0)),
            pl.BlockSpec((PROJ_TM, F_PAD), lambda i, j: (i, 0)),
        ],
        out_shape=[
            jax.ShapeDtypeStruct((BATCH, N_QKV_HEADS, SEQ, LANES), BF16),
            jax.ShapeDtypeStruct((ROWS, F_PAD), F32),
        ],
        scratch_shapes=[pltpu.VMEM((PROJ_TM, D_MODEL), BF16)],
        compiler_params=pltpu.CompilerParams(
            dimension_semantics=("arbitrary", "arbitrary"), vmem_limit_bytes=VMEM_LIMIT),
        name="qkv_proj",
    )(x2, w, b, cs, padc, rc1, rc2, wf, bf)


def _zg_proj_kernel(x_ref, w_ref, b_ref, o_ref, xb_ref):
    j = pl.program_id(1)

    @pl.when(j == 0)
    def _():
        xb_ref[...] = x_ref[...].astype(BF16)

    acc = _dot(xb_ref[...], w_ref[...]) + b_ref[...]
    is_z = j < (3 * WIDTH) // ZG_TN
    o_ref[...] = (jnp.where(is_z, acc, 1.0) * _sigmoid(acc)).astype(BF16)


def _zg_proj(x2, w, b):
    return pl.pallas_call(
        _zg_proj_kernel,
        grid=(ROWS // PROJ_TM, N_ZG // ZG_TN),
        in_specs=[
            pl.BlockSpec((PROJ_TM, D_MODEL), lambda i, j: (i, 0)),
            pl.BlockSpec((D_MODEL, ZG_TN), lambda i, j: (0, j)),
            pl.BlockSpec((1, ZG_TN), lambda i, j: (0, j)),
        ],
        out_specs=pl.BlockSpec((PROJ_TM, ZG_TN), lambda i, j: (i, j)),
        out_shape=jax.ShapeDtypeStruct((ROWS, N_ZG), BF16),
        scratch_shapes=[pltpu.VMEM((PROJ_TM, D_MODEL), BF16)],
        compiler_params=pltpu.CompilerParams(
            dimension_semantics=("arbitrary", "arbitrary"), vmem_limit_bytes=VMEM_LIMIT),
        name="zg_proj",
    )(x2, w, b)


def _fox_pack_kernel(f_ref, k_ref, place_ref, o_ref, carry_ref):
    t = pl.program_id(1)

    @pl.when(t == 0)
    def _():
        carry_ref[...] = jnp.zeros_like(carry_ref)

    f = f_ref[...]
    log_f = jnp.minimum(f, 0.0) - jnp.log(1.0 + jnp.exp(-jnp.abs(f)))
    row = lax.broadcasted_iota(jnp.int32, (ATT_T, ATT_T), 0)
    col = lax.broadcasted_iota(jnp.int32, (ATT_T, ATT_T), 1)
    tri = (row >= col).astype(BF16)
    pieces = jnp.concatenate([p.astype(BF16) for p in _split3(log_f)], axis=1)
    sums = _dot(tri, pieces)
    cum = (sums[:, 0:F_PAD] + sums[:, F_PAD:2 * F_PAD] + sums[:, 2 * F_PAD:3 * F_PAD]) + carry_ref[...]
    carry_ref[...] = cum[ATT_T - 1:ATT_T, :]
    c_pieces = jnp.concatenate([p.astype(BF16) for p in _split3(cum * LOG2E)], axis=1)
    placed = _dot(c_pieces, place_ref[...])
    for h in range(N_HEADS):
        o_ref[0, h] = (k_ref[0, h].astype(F32) + placed[:, h * LANES:(h + 1) * LANES]).astype(BF16)


def _fox_pack(f_raw, qkv, place):
    k_block = HEAD_SLOT["fox_k"] // N_HEADS
    return pl.pallas_call(
        _fox_pack_kernel,
        grid=(BATCH, N_ATT_TILES),
        in_specs=[
            pl.BlockSpec((ATT_T, F_PAD), lambda b, t: (b * N_ATT_TILES + t, 0)),
            pl.BlockSpec((1, N_HEADS, ATT_T, LANES), lambda b, t: (b, k_block, t, 0)),
            pl.BlockSpec((BIAS_SPLIT * F_PAD, N_HEADS * LANES), lambda b, t: (0, 0)),
        ],
        out_specs=pl.BlockSpec((1, N_HEADS, ATT_T, LANES), lambda b, t: (b, 0, t, 0)),
        out_shape=jax.ShapeDtypeStruct((BATCH, N_HEADS, SEQ, LANES), BF16),
        scratch_shapes=[pltpu.VMEM((1, F_PAD), F32)],
        compiler_params=pltpu.CompilerParams(dimension_semantics=("arbitrary", "arbitrary")),
        name="fox_pack",
    )(f_raw, qkv, place)


def _row_norm_max(x):
    lane = lax.broadcasted_iota(jnp.int32, (1, LANES), 1)
    xf = jnp.where(lane < HEAD_DIM, x.astype(F32), 0.0)
    return jnp.sqrt(jnp.max(jnp.sum(xf * xf, axis=1, keepdims=True), axis=0, keepdims=True))


def _store_key_norm_max(k_ref, kmax_ref):
    for hh in range(ATT_HEADS):
        def chunk(c, best):
            rows = pl.ds(pl.multiple_of(c * ATT_T, ATT_T), ATT_T)
            return jnp.maximum(best, _row_norm_max(k_ref[0, hh, rows, :]))
        best = lax.fori_loop(0, N_ATT_TILES, chunk, jnp.zeros((1, 1), F32))
        kmax_ref[hh] = jnp.broadcast_to(best, kmax_ref.shape[1:])


def _flash_heads(q_aug, k_ref, v_ref, i, key_extra=None, group_offset=None, older_bound=None):
    t = ATT_T

    def step(start, width, group, carry, causal):
        offs = None if group_offset is None else group_offset(group)
        extra = None if key_extra is None else key_extra(start, width)
        scores = []
        for hh in range(ATT_HEADS):
            ks = k_ref[0, hh, pl.ds(start, width), :]
            if extra is not None:
                ks = jnp.concatenate([ks, extra], axis=1)
            scores.append(_dot_nt(q_aug[hh], ks))
        new = []
        for hh in range(ATT_HEADS):
            m, acc = carry[hh]
            s = scores[hh]
            if causal:
                row = lax.broadcasted_iota(jnp.int32, (t, width), 0)
                col = lax.broadcasted_iota(jnp.int32, (t, width), 1)
                s = jnp.where(col <= row, s, NEG_INF)
            tile_max = jnp.max(s, axis=1, keepdims=True)
            if offs is not None:
                tile_max = tile_max + offs[hh]
            m_new = jnp.maximum(m, tile_max)
            shift = m_new if offs is None else m_new - offs[hh]
            alpha = jnp.exp2(m - m_new)
            p = jnp.exp2(s - shift).astype(BF16)
            acc = alpha * acc + _dot(p, v_ref[0, hh, pl.ds(start, width), :])
            new.append((m_new, acc))
        return tuple(new)

    def still_needed(g, carry):
        if older_bound is None:
            return jnp.int32(1)
        worst = None
        for hh in range(ATT_HEADS):
            gap = older_bound(hh, jnp.maximum(g, 0)) - carry[hh][0]
            worst = gap if worst is None else jnp.maximum(worst, gap)
        return (jnp.max(worst) > -UNDERFLOW_BITS).astype(jnp.int32)

    own_group = i // (ATT_GROUP // t)
    carry = tuple((jnp.full((t, 1), NEG_INF, F32), jnp.zeros((t, LANES), F32)) for _ in range(ATT_HEADS))
    carry = step(pl.multiple_of(i * t, t), t, own_group, carry, True)
    carry = lax.fori_loop(
        own_group * (ATT_GROUP // t), i,
        lambda j, c: step(pl.multiple_of(j * t, t), t, own_group, c, False), carry)

    def older_group(state):
        g, _, c = state
        needed_next = still_needed(g - 1, c)
        c = step(pl.multiple_of(g * ATT_GROUP, ATT_GROUP), ATT_GROUP, g, c, False)
        return g - 1, needed_next, c

    _, _, carry = lax.while_loop(
        lambda state: jnp.logical_and(state[0] >= 0, state[1] == 1), older_group,
        (own_group - 1, still_needed(own_group - 1, carry), carry))
    return [acc / acc[:, ONES_LANE:ONES_LANE + 1] for _, acc in carry]


def _store_heads(o_ref, outs):
    lane = lax.broadcasted_iota(jnp.int32, (ATT_T, LANES), 1)
    for pair in range(ATT_HEADS // 2):
        both = jnp.where(lane < HEAD_DIM, outs[2 * pair], pltpu.roll(outs[2 * pair + 1], HEAD_DIM, axis=1))
        o_ref[0, :, pair * LANES:(pair + 1) * LANES] = both.astype(BF16)


def _attention_specs(q_slot, k_index, v_slot):
    q0, v0 = q_slot // ATT_HEADS, v_slot // ATT_HEADS
    resident = lambda index: pl.BlockSpec((1, ATT_HEADS, SEQ, LANES), index, pipeline_mode=pl.Buffered(1))
    in_specs = [
        pl.BlockSpec((1, ATT_HEADS, ATT_T, LANES), lambda b, hg, i: (b, q0 + hg, i, 0)),
        resident(lambda b, hg, i: (b, k_index + hg, 0, 0)),
        resident(lambda b, hg, i: (b, v0 + hg, 0, 0)),
    ]
    out_spec = pl.BlockSpec((1, ATT_T, ATT_HEADS * HEAD_DIM), lambda b, hg, i: (b, i, hg))
    return in_specs, out_spec


def _fox_kernel(q_ref, k_ref, v_ref, o_ref, kmax_ref):
    i = pl.program_id(2)

    @pl.when(i == 0)
    def _():
        _store_key_norm_max(k_ref, kmax_ref)

    q = [q_ref[0, hh] for hh in range(ATT_HEADS)]
    qk_bound = [_row_norm_max(q[hh]) * kmax_ref[hh][0:1, 0:1] + BOUND_SLACK for hh in range(ATT_HEADS)]
    lane = lax.broadcasted_iota(jnp.int32, (1, LANES), 1)
    bias_lanes = (lane >= HEAD_DIM) & (lane < HEAD_DIM + BIAS_SPLIT)
    tail = 16

    def older_bound(hh, g):
        rows = pl.ds(pl.multiple_of((g + 1) * ATT_GROUP - tail, tail), tail)
        bias = jnp.sum(jnp.where(bias_lanes, k_ref[0, hh, rows, :].astype(F32), 0.0), axis=1, keepdims=True)
        return qk_bound[hh] + jnp.max(bias, axis=0, keepdims=True)

    outs = _flash_heads(q, k_ref, v_ref, i, older_bound=older_bound)
    _store_heads(o_ref, outs)


def _fox_attention(qkv, k_aug):
    in_specs, out_spec = _attention_specs(HEAD_SLOT["fox_q"], 0, HEAD_SLOT["fox_v"])
    return pl.pallas_call(
        _fox_kernel,
        grid=(BATCH, N_HEADS // ATT_HEADS, N_ATT_TILES),
        in_specs=in_specs,
        out_specs=out_spec,
        out_shape=jax.ShapeDtypeStruct((BATCH, SEQ, WIDTH), BF16),
        scratch_shapes=[pltpu.VMEM((ATT_HEADS, 8, LANES), F32)],
        compiler_params=pltpu.CompilerParams(
            dimension_semantics=("arbitrary", "arbitrary", "arbitrary"), vmem_limit_bytes=VMEM_LIMIT),
        name="fox_attention",
    )(qkv, k_aug, qkv)


def _swa_kernel(sinks_ref, slopes_ref, q_ref, k_ref, v_ref, o_ref):
    hkv = pl.program_id(1)
    i = pl.program_id(2)
    w = SWA_WINDOW
    qi = lax.broadcasted_iota(jnp.int32, (w, 2 * w), 0)
    ki = lax.broadcasted_iota(jnp.int32, (w, 2 * w), 1)
    for sub in range(SWA_TQ // w):
        q_start = i * SWA_TQ + sub * w
        k_start = pl.multiple_of(jnp.maximum(q_start - w, 0), w)
        ks = k_ref[0, 0, pl.ds(k_start, 2 * w), :]
        vs = v_ref[0, 0, pl.ds(k_start, 2 * w), :]
        rel = (q_start - k_start) + qi - ki
        valid = (rel >= 0) & (rel < w)
        rel_f = rel.astype(F32)
        for g in range(SWA_GROUP):
            h = hkv * SWA_GROUP + g
            q = q_ref[0, g, sub * w:(sub + 1) * w, :] * jnp.asarray(SCALE, BF16)
            s = _dot_nt(q, ks) - slopes_ref[h] * rel_f
            s = jnp.where(valid, s, NEG_INF)
            sink = sinks_ref[h]
            m = jnp.maximum(jnp.max(s, axis=1, keepdims=True), sink)
            e = jnp.exp(s - m)
            denom = jnp.sum(e, axis=1, keepdims=True) + jnp.exp(sink - m)
            out = _dot((e / denom).astype(BF16), vs)[:, 0:HEAD_DIM]
            o_ref[0, sub * w:(sub + 1) * w, g * HEAD_DIM:(g + 1) * HEAD_DIM] = out.astype(BF16)


def _swa_attention(qkv, sinks, slopes):
    q0 = HEAD_SLOT["swa_q"] // SWA_GROUP
    k0, v0 = HEAD_SLOT["swa_k"], HEAD_SLOT["swa_v"]
    smem = pl.BlockSpec(memory_space=pltpu.SMEM)
    return pl.pallas_call(
        _swa_kernel,
        grid=(BATCH, SWA_KV_HEADS, SEQ // SWA_TQ),
        in_specs=[
            smem, smem,
            pl.BlockSpec((1, SWA_GROUP, SWA_TQ, LANES), lambda b, hk, i: (b, q0 + hk, i, 0)),
            pl.BlockSpec((1, 1, SEQ, LANES), lambda b, hk, i: (b, k0 + hk, 0, 0)),
            pl.BlockSpec((1, 1, SEQ, LANES), lambda b, hk, i: (b, v0 + hk, 0, 0)),
        ],
        out_specs=pl.BlockSpec((1, SWA_TQ, SWA_GROUP * HEAD_DIM), lambda b, hk, i: (b, i, hk)),
        out_shape=jax.ShapeDtypeStruct((BATCH, SEQ, WIDTH), BF16),
        compiler_params=pltpu.CompilerParams(
            dimension_semantics=("arbitrary", "arbitrary", "arbitrary"), vmem_limit_bytes=VMEM_LIMIT),
        name="swa_attention",
    )(sinks, slopes, qkv, qkv, qkv)


def _moba_kernel(slope2_ref, q_ref, k_ref, v_ref, o_ref, kmean_ref, onehot_ref, kmax_ref):
    hg = pl.program_id(1)
    i = pl.program_id(2)
    t = ATT_T
    blocks_per_tile = t // MOBA_BLOCK
    row_block = lax.broadcasted_iota(jnp.int32, (t, LANES), 0) // MOBA_BLOCK
    blk = lax.broadcasted_iota(jnp.int32, (t, LANES), 1)

    @pl.when(i == 0)
    def _():
        _store_key_norm_max(k_ref, kmax_ref)
        lane = lax.broadcasted_iota(jnp.int32, (1, LANES), 1)
        for hh in range(ATT_HEADS):
            def block_mean(bk, _):
                start = pl.multiple_of(bk * MOBA_BLOCK, MOBA_BLOCK)
                kb = k_ref[0, hh, pl.ds(start, MOBA_BLOCK), :].astype(F32)
                mean = jnp.sum(kb, axis=0, keepdims=True) * (1.0 / MOBA_BLOCK)
                kmean_ref[hh, pl.ds(bk, 1), :] = jnp.where(lane < HEAD_DIM, mean, 0.0)
                return 0
            lax.fori_loop(0, N_MOBA_BLOCKS, block_mean, 0)

        def indicator(j, _):
            rows = pl.ds(pl.multiple_of(j * t, t), t)
            onehot_ref[rows, :] = (blk == blocks_per_tile * j + row_block).astype(BF16)
            return 0
        lax.fori_loop(0, N_ATT_TILES, indicator, 0)

    blk_t = lax.broadcasted_iota(jnp.int32, (N_MOBA_BLOCKS, t), 0)
    own_t = blocks_per_tile * i + lax.broadcasted_iota(jnp.int32, (N_MOBA_BLOCKS, t), 1) // MOBA_BLOCK
    past = blk_t < own_t
    blk_f = blk_t.astype(F32)
    never = jnp.full((LANES - N_MOBA_BLOCKS, t), NEG_INF, F32)
    q_aug = []
    for hh in range(ATT_HEADS):
        q = q_ref[0, hh]
        gate = _dot_nt(kmean_ref[hh].astype(BF16), q)
        gate = jnp.where(past, gate, NEG_INF)
        sel = blk_t == own_t
        for _ in range(MOBA_TOPK):
            mx = jnp.max(gate, axis=0, keepdims=True)
            first = jnp.min(jnp.where(gate == mx, blk_f, float(N_MOBA_BLOCKS)), axis=0, keepdims=True)
            pick = blk_f == first
            sel = sel | (pick & past)
            gate = jnp.where(pick, -jnp.inf, gate)
        select_bias = jnp.concatenate([jnp.where(sel, 0.0, NEG_INF), never], axis=0)
        q_aug.append(jnp.concatenate([q, select_bias.T.astype(BF16)], axis=1))

    def group_offset(g):
        keys_back = ((i // (ATT_GROUP // t) - g) * ATT_GROUP).astype(F32)
        return [-(slope2_ref[ATT_HEADS * hg + hh] * keys_back) for hh in range(ATT_HEADS)]

    qk_bound = [_row_norm_max(q_ref[0, hh]) * kmax_ref[hh][0:1, 0:1] + BOUND_SLACK for hh in range(ATT_HEADS)]

    def older_bound(hh, g):
        return qk_bound[hh] + group_offset(g)[hh]

    outs = _flash_heads(q_aug, k_ref, v_ref, i, group_offset=group_offset, older_bound=older_bound,
                        key_extra=lambda start, width: onehot_ref[pl.ds(start, width), :])
    _store_heads(o_ref, outs)


def _moba_attention(qkv, slope2):
    in_specs, out_spec = _attention_specs(
        HEAD_SLOT["moba_q"], HEAD_SLOT["moba_k"] // ATT_HEADS, HEAD_SLOT["moba_v"])
    return pl.pallas_call(
        _moba_kernel,
        grid=(BATCH, N_HEADS // ATT_HEADS, N_ATT_TILES),
        in_specs=[pl.BlockSpec(memory_space=pltpu.SMEM)] + in_specs,
        out_specs=out_spec,
        out_shape=jax.ShapeDtypeStruct((BATCH, SEQ, WIDTH), BF16),
        scratch_shapes=[pltpu.VMEM((ATT_HEADS, N_MOBA_BLOCKS, LANES), F32),
                        pltpu.VMEM((SEQ, LANES), BF16),
                        pltpu.VMEM((ATT_HEADS, 8, LANES), F32)],
        compiler_params=pltpu.CompilerParams(
            dimension_semantics=("arbitrary", "arbitrary", "arbitrary"), vmem_limit_bytes=VMEM_LIMIT),
        name="moba_attention",
    )(slope2, qkv, qkv, qkv)


def _merge_kernel(x_ref, zg_ref, ofox_ref, oswa_ref, omoba_ref, wbr_ref, wout_ref, g_ref, b_ref, o_ref):
    y = jnp.zeros((MERGE_TM, D_MODEL), F32)
    for br, o_br in enumerate((ofox_ref, oswa_ref, omoba_ref)):
        silu_z = zg_ref[:, br * WIDTH:(br + 1) * WIDTH].astype(F32)
        a = (o_br[...].astype(F32) * silu_z).astype(BF16)
        gate = zg_ref[:, 3 * WIDTH + br * D_MODEL:3 * WIDTH + (br + 1) * D_MODEL].astype(F32)
        y = y + gate * _dot(a, wbr_ref[br])
    out = _dot(y.astype(BF16), wout_ref[...])
    r = DEEPNORM_ALPHA * x_ref[...] + out
    mu = jnp.mean(r, axis=1, keepdims=True)
    d = r - mu
    var = jnp.mean(d * d, axis=1, keepdims=True)
    o_ref[...] = d * lax.rsqrt(var + LN_EPS) * g_ref[...] + b_ref[...]


def _merge(x2, zg, o_fox, o_swa, o_moba, w_br, w_out, ln_g, ln_b):
    row_tile = lambda n: pl.BlockSpec((MERGE_TM, n), lambda i: (i, 0))
    fixed = lambda shape: pl.BlockSpec(shape, lambda i: (0,) * len(shape), pipeline_mode=pl.Buffered(1))
    return pl.pallas_call(
        _merge_kernel,
        grid=(ROWS // MERGE_TM,),
        in_specs=[
            row_tile(D_MODEL), row_tile(N_ZG), row_tile(WIDTH), row_tile(WIDTH), row_tile(WIDTH),
            fixed((3, WIDTH, D_MODEL)), fixed((D_MODEL, D_MODEL)), fixed((1, D_MODEL)), fixed((1, D_MODEL)),
        ],
        out_specs=row_tile(D_MODEL),
        out_shape=jax.ShapeDtypeStruct((ROWS, D_MODEL), F32),
        compiler_params=pltpu.CompilerParams(
            dimension_semantics=("arbitrary",), vmem_limit_bytes=VMEM_LIMIT),
        name="merge_deepnorm",
    )(x2, zg, o_fox, o_swa, o_moba, w_br, w_out, ln_g, ln_b)


def _columns(w, names):
    return jnp.concatenate([w[..., _OFF[n][0]:_OFF[n][1]] for n in names], axis=-1)


def _alibi_slopes(n):
    return jnp.power(2.0, -8.0 * jnp.arange(1, n + 1, dtype=F32) / n)


def _bias_lane_tables():
    cs = jnp.ones((N_QKV_HEADS, HEAD_DIM), F32)
    padc = jnp.zeros((N_QKV_HEADS, LANES), F32)
    rc1 = jnp.zeros((N_QKV_HEADS, LANES), F32)
    rc2 = jnp.zeros((N_QKV_HEADS, LANES), F32)
    heads = lambda name: slice(HEAD_SLOT[name], HEAD_SLOT[name] + N_HEADS)
    b0 = HEAD_DIM
    s3 = BIAS_SPLIT
    cs = cs.at[heads("fox_q")].set(SCALE * LOG2E).at[heads("moba_q")].set(SCALE * LOG2E)
    padc = padc.at[heads("fox_q"), b0:b0 + s3].set(1.0)
    padc = padc.at[heads("fox_v"), ONES_LANE].set(1.0).at[heads("moba_v"), ONES_LANE].set(1.0)
    slope_pieces = jnp.stack(_split3(_alibi_slopes(N_HEADS) * LOG2E), axis=1)
    padc = padc.at[heads("moba_q"), b0:b0 + s3].set(slope_pieces)
    padc = padc.at[heads("moba_q"), b0 + s3:b0 + 2 * s3].set(slope_pieces)
    rc1 = rc1.at[heads("moba_k"), b0:b0 + s3].set(1.0)
    rc2 = rc2.at[heads("moba_k"), b0 + s3:b0 + 2 * s3].set(1.0)
    flat = lambda a: a.reshape(1, -1)
    return flat(cs), flat(padc), flat(rc1), flat(rc2)


def _fox_place_matrix():
    place = jnp.zeros((BIAS_SPLIT * F_PAD, N_HEADS * LANES), F32)
    for p in range(BIAS_SPLIT):
        for h in range(N_HEADS):
            place = place.at[p * F_PAD + h, h * LANES + HEAD_DIM + p].set(-1.0)
    return place.astype(BF16)


def _layer(x2, w_in, b_in, sinks, w_br, w_out, ln_g, ln_b, tables, place):
    zg_names = ("fox_z", "swa_z", "moba_z", "gate_fox", "gate_swa", "gate_moba")
    w_qkv = _columns(w_in, QKV_ORDER).astype(BF16)
    b_qkv = _columns(b_in, QKV_ORDER)[None, :]
    w_f = jnp.pad(_columns(w_in, ("fox_f",)), ((0, 0), (0, F_PAD - N_HEADS))).astype(BF16)
    b_f = jnp.pad(_columns(b_in, ("fox_f",)), (0, F_PAD - N_HEADS))[None, :]
    w_zg = _columns(w_in, zg_names).astype(BF16)
    b_zg = _columns(b_in, zg_names)[None, :]

    qkv, f_raw = _qkv_proj(x2, w_qkv, b_qkv, *tables, w_f, b_f)
    zg = _zg_proj(x2, w_zg, b_zg)
    k_fox = _fox_pack(f_raw, qkv, place)
    o_fox = _fox_attention(qkv, k_fox).reshape(ROWS, WIDTH)
    o_swa = _swa_attention(qkv, sinks, _alibi_slopes(N_HEADS)).reshape(ROWS, WIDTH)
    o_moba = _moba_attention(qkv, _alibi_slopes(N_HEADS) * LOG2E).reshape(ROWS, WIDTH)
    return _merge(x2, zg, o_fox, o_swa, o_moba, w_br.astype(BF16), w_out.astype(BF16),
                  ln_g[None, :], ln_b[None, :])


def kernel(x, w_in, b_in, swa_sinks, w_branch_fox, w_branch_swa, w_branch_moba, w_out, ln_gain, ln_bias):
    x2 = x.reshape(ROWS, D_MODEL)
    tables = _bias_lane_tables()
    place = _fox_place_matrix()
    for l in range(DEPTH):
        w_br = jnp.stack([w_branch_fox[l], w_branch_swa[l], w_branch_moba[l]])
        x2 = _layer(x2, w_in[l], b_in[l], swa_sinks[l], w_br, w_out[l], ln_gain[l], ln_bias[l], tables, place)
    return x2.reshape(BATCH, SEQ, D_MODEL)
```

```python
import math

import jax
import jax.numpy as jnp
from jax import lax
from jax.experimental import pallas as pl
from jax.experimental.pallas import tpu as pltpu

D_MODEL = 2048
BATCH = 2
SEQ = 16384
DEPTH = 2
HEAD_DIM = 64
LANES = 128
N_HEADS = 8
SWA_KV_HEADS = 2
SWA_GROUP = N_HEADS // SWA_KV_HEADS
WIDTH = N_HEADS * HEAD_DIM
SWA_WINDOW = 128
MOBA_BLOCK = 256
MOBA_TOPK = 3
N_MOBA_BLOCKS = SEQ // MOBA_BLOCK
DEEPNORM_ALPHA = (2.0 * DEPTH) ** 0.25
LN_EPS = 1e-5
NEG_INF = -1e30
SCALE = HEAD_DIM ** -0.5
LOG2E = math.log2(math.e)

_SEG = (("fox_q", WIDTH), ("fox_k", WIDTH), ("fox_v", WIDTH), ("fox_z", WIDTH), ("fox_f", N_HEADS),
        ("swa_q", WIDTH), ("swa_k", SWA_KV_HEADS * HEAD_DIM), ("swa_v", SWA_KV_HEADS * HEAD_DIM), ("swa_z", WIDTH),
        ("moba_q", WIDTH), ("moba_k", WIDTH), ("moba_v", WIDTH), ("moba_z", WIDTH),
        ("gate_fox", D_MODEL), ("gate_swa", D_MODEL), ("gate_moba", D_MODEL))
_OFF = {}
_start = 0
for _name, _size in _SEG:
    _OFF[_name] = (_start, _start + _size)
    _start += _size

QKV_ORDER = ("fox_q", "fox_k", "fox_v", "swa_q", "swa_k", "swa_v", "moba_q", "moba_k", "moba_v")
HEAD_SLOT = {}
_slot = 0
for _name in QKV_ORDER:
    HEAD_SLOT[_name] = _slot
    _slot += (_OFF[_name][1] - _OFF[_name][0]) // HEAD_DIM
N_QKV_HEADS = _slot
N_QKV = N_QKV_HEADS * HEAD_DIM
N_ZG = 3 * WIDTH + 3 * D_MODEL
F_PAD = 128

ROWS = BATCH * SEQ
PROJ_TM = 1024
QKV_TN = 1280
ZG_TN = 1536
ATT_T = 512
ATT_GROUP = 1024
ATT_HEADS = 4
NORM_CHUNK = 2048
N_ATT_TILES = SEQ // ATT_T
ROW_BIAS_HEADS = frozenset(
    (HEAD_SLOT["moba_k"] + h) % (QKV_TN // HEAD_DIM) for h in range(N_HEADS))
SWA_TQ = 512
MERGE_TM = 512
VMEM_LIMIT = 56 * 1024 * 1024

BIAS_SPLIT = 3
ONES_LANE = HEAD_DIM
UNDERFLOW_BITS = 152.0
BOUND_SLACK = 8.0

F32 = jnp.float32
BF16 = jnp.bfloat16


def _dot(a, b):
    return jnp.dot(a, b, preferred_element_type=F32)


def _dot_nt(a, b):
    return lax.dot_general(a, b, (((1,), (1,)), ((), ())), preferred_element_type=F32)


def _sigmoid(v):
    return 1.0 / (1.0 + jnp.exp(-v))


def _split3(v):
    hi = v.astype(BF16).astype(F32)
    rem = v - hi
    mid = rem.astype(BF16).astype(F32)
    lo = (rem - mid).astype(BF16).astype(F32)
    return hi, mid, lo


def _qkv_proj_kernel(x_ref, w_ref, b_ref, cs_ref, padc_ref, rc1_ref, rc2_ref, wf_ref, bf_ref,
                     o_ref, f_ref, xb_ref):
    j = pl.program_id(1)

    @pl.when(j == 0)
    def _():
        xb_ref[...] = x_ref[...].astype(BF16)
        f_ref[...] = _dot(xb_ref[...], wf_ref[...]) + bf_ref[...]

    acc = (_dot(xb_ref[...], w_ref[...]) + b_ref[...]) * cs_ref[...]
    r = lax.broadcasted_iota(jnp.int32, (PROJ_TM, 1), 0)
    row1 = ((r % MOBA_BLOCK) - (MOBA_BLOCK - 1)).astype(F32)
    row2 = (((r // MOBA_BLOCK) % (ATT_GROUP // MOBA_BLOCK)) * MOBA_BLOCK - (ATT_GROUP - MOBA_BLOCK)).astype(F32)
    lane = lax.broadcasted_iota(jnp.int32, (PROJ_TM, LANES), 1)
    data_lane = lane < HEAD_DIM
    for pair in range(QKV_TN // LANES):
        both = acc[:, pair * LANES:(pair + 1) * LANES]
        swapped = pltpu.roll(both, HEAD_DIM, axis=1)
        for hh, data in ((2 * pair, both), (2 * pair + 1, swapped)):
            cols = slice(hh * LANES, (hh + 1) * LANES)
            bias = padc_ref[:, cols]
            if hh in ROW_BIAS_HEADS:
                bias = bias + rc1_ref[:, cols] * row1 + rc2_ref[:, cols] * row2
            o_ref[0, hh] = jnp.where(data_lane, data, bias).astype(BF16)


def _qkv_proj(x2, w, b, cs, padc, rc1, rc2, wf, bf):
    tiles_per_batch = SEQ // PROJ_TM
    heads_per_step = QKV_TN // HEAD_DIM
    col = lambda n: pl.BlockSpec((1, n), lambda i, j: (0, j))
    return pl.pallas_call(
        _qkv_proj_kernel,
        grid=(ROWS // PROJ_TM, N_QKV // QKV_TN),
        in_specs=[
            pl.BlockSpec((PROJ_TM, D_MODEL), lambda i, j: (i, 0)),
            pl.BlockSpec((D_MODEL, QKV_TN), lambda i, j: (0, j)),
            col(QKV_TN), col(QKV_TN),
            col(heads_per_step * LANES), col(heads_per_step * LANES), col(heads_per_step * LANES),
            pl.BlockSpec((D_MODEL, F_PAD), lambda i, j: (0, 0)),
            pl.BlockSpec((1, F_PAD), lambda i, j: (0, 0)),
        ],
        out_specs=[
            pl.BlockSpec((1, heads_per_step, PROJ_TM, LANES),
                         lambda i, j: (i // tiles_per_batch, j, i % tiles_per_batch, 0)),
            pl.BlockSpec((PROJ_TM, F_PAD), lambda i, j: (i, 0)),
        ],
        out_shape=[
            jax.ShapeDtypeStruct((BATCH, N_QKV_HEADS, SEQ, LANES), BF16),
            jax.ShapeDtypeStruct((ROWS, F_PAD), F32),
        ],
        scratch_shapes=[pltpu.VMEM((PROJ_TM, D_MODEL), BF16)],
        compiler_params=pltpu.CompilerParams(
            dimension_semantics=("arbitrary", "arbitrary"), vmem_limit_bytes=VMEM_LIMIT),
        name="qkv_proj",
    )(x2, w, b, cs, padc, rc1, rc2, wf, bf)


def _zg_proj_kernel(x_ref, w_ref, b_ref, o_ref, xb_ref):
    j = pl.program_id(1)

    @pl.when(j == 0)
    def _():
        xb_ref[...] = x_ref[...].astype(BF16)

    acc = _dot(xb_ref[...], w_ref[...]) + b_ref[...]
    is_z = j < (3 * WIDTH) // ZG_TN
    o_ref[...] = (jnp.where(is_z, acc, 1.0) * _sigmoid(acc)).astype(BF16)


def _zg_proj(x2, w, b):
    return pl.pallas_call(
        _zg_proj_kernel,
        grid=(ROWS // PROJ_TM, N_ZG // ZG_TN),
        in_specs=[
            pl.BlockSpec((PROJ_TM, D_MODEL), lambda i, j: (i, 0)),
            pl.BlockSpec((D_MODEL, ZG_TN), lambda i, j: (0, j)),
            pl.BlockSpec((1, ZG_TN), lambda i, j: (0, j)),
        ],
        out_specs=pl.BlockSpec((PROJ_TM, ZG_TN), lambda i, j: (i, j)),
        out_shape=jax.ShapeDtypeStruct((ROWS, N_ZG), BF16),
        scratch_shapes=[pltpu.VMEM((PROJ_TM, D_MODEL), BF16)],
        compiler_params=pltpu.CompilerParams(
            dimension_semantics=("arbitrary", "arbitrary"), vmem_limit_bytes=VMEM_LIMIT),
        name="zg_proj",
    )(x2, w, b)


def _fox_pack_kernel(f_ref, k_ref, place_ref, o_ref, carry_ref):
    t = pl.program_id(1)

    @pl.when(t == 0)
    def _():
        carry_ref[...] = jnp.zeros_like(carry_ref)

    f = f_ref[...]
    log_f = jnp.minimum(f, 0.0) - jnp.log(1.0 + jnp.exp(-jnp.abs(f)))
    row = lax.broadcasted_iota(jnp.int32, (ATT_T, ATT_T), 0)
    col = lax.broadcasted_iota(jnp.int32, (ATT_T, ATT_T), 1)
    tri = (row >= col).astype(BF16)
    pieces = jnp.concatenate([p.astype(BF16) for p in _split3(log_f)], axis=1)
    sums = _dot(tri, pieces)
    cum = (sums[:, 0:F_PAD] + sums[:, F_PAD:2 * F_PAD] + sums[:, 2 * F_PAD:3 * F_PAD]) + carry_ref[...]
    carry_ref[...] = cum[ATT_T - 1:ATT_T, :]
    c_pieces = jnp.concatenate([p.astype(BF16) for p in _split3(cum * LOG2E)], axis=1)
    placed = _dot(c_pieces, place_ref[...])
    for h in range(N_HEADS):
        o_ref[0, h] = (k_ref[0, h].astype(F32) + placed[:, h * LANES:(h + 1) * LANES]).astype(BF16)


def _fox_pack(f_raw, qkv, place):
    k_block = HEAD_SLOT["fox_k"] // N_HEADS
    return pl.pallas_call(
        _fox_pack_kernel,
        grid=(BATCH, N_ATT_TILES),
        in_specs=[
            pl.BlockSpec((ATT_T, F_PAD), lambda b, t: (b * N_ATT_TILES + t, 0)),
            pl.BlockSpec((1, N_HEADS, ATT_T, LANES), lambda b, t: (b, k_block, t, 0)),
            pl.BlockSpec((BIAS_SPLIT * F_PAD, N_HEADS * LANES), lambda b, t: (0, 0)),
        ],
        out_specs=pl.BlockSpec((1, N_HEADS, ATT_T, LANES), lambda b, t: (b, 0, t, 0)),
        out_shape=jax.ShapeDtypeStruct((BATCH, N_HEADS, SEQ, LANES), BF16),
        scratch_shapes=[pltpu.VMEM((1, F_PAD), F32)],
        compiler_params=pltpu.CompilerParams(dimension_semantics=("arbitrary", "arbitrary")),
        name="fox_pack",
    )(f_raw, qkv, place)


def _row_sq_norm_max(x):
    lane = lax.broadcasted_iota(jnp.int32, (1, LANES), 1)
    xf = jnp.where(lane < HEAD_DIM, x.astype(F32), 0.0)
    return jnp.max(jnp.sum(xf * xf, axis=1, keepdims=True), axis=0, keepdims=True)


def _row_norm_max(x):
    return jnp.sqrt(_row_sq_norm_max(x))


def _store_key_norm_max(k_ref, kmax_ref):
    for hh in range(ATT_HEADS):
        def chunk(c, best):
            rows = pl.ds(pl.multiple_of(c * NORM_CHUNK, NORM_CHUNK), NORM_CHUNK)
            return jnp.maximum(best, _row_sq_norm_max(k_ref[0, hh, rows, :]))
        best = lax.fori_loop(0, SEQ // NORM_CHUNK, chunk, jnp.zeros((1, 1), F32))
        kmax_ref[hh] = jnp.broadcast_to(jnp.sqrt(best), kmax_ref.shape[1:])


def _flash_heads(q_aug, k_ref, v_ref, i, key_extra=None, group_offset=None, older_bound=None):
    t = ATT_T

    def step(start, width, group, carry, causal):
        offs = None if group_offset is None else group_offset(group)
        extra = None if key_extra is None else key_extra(start, width)
        scores = []
        for hh in range(ATT_HEADS):
            ks = k_ref[0, hh, pl.ds(start, width), :]
            if extra is not None:
                ks = jnp.concatenate([ks, extra], axis=1)
            scores.append(_dot_nt(q_aug[hh], ks))
        new = []
        for hh in range(ATT_HEADS):
            m, acc = carry[hh]
            s = scores[hh]
            if causal:
                row = lax.broadcasted_iota(jnp.int32, (t, width), 0)
                col = lax.broadcasted_iota(jnp.int32, (t, width), 1)
                s = jnp.where(col <= row, s, NEG_INF)
            tile_max = jnp.max(s, axis=1, keepdims=True)
            if offs is not None:
                tile_max = tile_max + offs[hh]
            m_new = jnp.maximum(m, tile_max)
            shift = m_new if offs is None else m_new - offs[hh]
            alpha = jnp.exp2(m - m_new)
            p = jnp.exp2(s - shift).astype(BF16)
            acc = alpha * acc + _dot(p, v_ref[0, hh, pl.ds(start, width), :])
            new.append((m_new, acc))
        return tuple(new)

    def still_needed(g, carry):
        if older_bound is None:
            return jnp.int32(1)
        worst = None
        for hh in range(ATT_HEADS):
            gap = older_bound(hh, jnp.maximum(g, 0)) - carry[hh][0]
            worst = gap if worst is None else jnp.maximum(worst, gap)
        return (jnp.max(worst) > -UNDERFLOW_BITS).astype(jnp.int32)

    own_group = i // (ATT_GROUP // t)
    carry = tuple((jnp.full((t, 1), NEG_INF, F32), jnp.zeros((t, LANES), F32)) for _ in range(ATT_HEADS))
    carry = step(pl.multiple_of(i * t, t), t, own_group, carry, True)
    carry = lax.fori_loop(
        own_group * (ATT_GROUP // t), i,
        lambda j, c: step(pl.multiple_of(j * t, t), t, own_group, c, False), carry)

    def older_group(state):
        g, _, c = state
        needed_next = still_needed(g - 1, c)
        c = step(pl.multiple_of(g * ATT_GROUP, ATT_GROUP), ATT_GROUP, g, c, False)
        return g - 1, needed_next, c

    _, _, carry = lax.while_loop(
        lambda state: jnp.logical_and(state[0] >= 0, state[1] == 1), older_group,
        (own_group - 1, still_needed(own_group - 1, carry), carry))
    return [acc / acc[:, ONES_LANE:ONES_LANE + 1] for _, acc in carry]


def _store_heads(o_ref, outs):
    lane = lax.broadcasted_iota(jnp.int32, (ATT_T, LANES), 1)
    for pair in range(ATT_HEADS // 2):
        both = jnp.where(lane < HEAD_DIM, outs[2 * pair], pltpu.roll(outs[2 * pair + 1], HEAD_DIM, axis=1))
        o_ref[0, :, pair * LANES:(pair + 1) * LANES] = both.astype(BF16)


def _attention_specs(q_slot, k_index, v_slot):
    q0, v0 = q_slot // ATT_HEADS, v_slot // ATT_HEADS
    resident = lambda index: pl.BlockSpec((1, ATT_HEADS, SEQ, LANES), index, pipeline_mode=pl.Buffered(1))
    in_specs = [
        pl.BlockSpec((1, ATT_HEADS, ATT_T, LANES), lambda b, hg, i: (b, q0 + hg, i, 0)),
        resident(lambda b, hg, i: (b, k_index + hg, 0, 0)),
        resident(lambda b, hg, i: (b, v0 + hg, 0, 0)),
    ]
    out_spec = pl.BlockSpec((1, ATT_T, ATT_HEADS * HEAD_DIM), lambda b, hg, i: (b, i, hg))
    return in_specs, out_spec


def _fox_kernel(q_ref, k_ref, v_ref, o_ref, kmax_ref):
    i = pl.program_id(2)

    @pl.when(i == 0)
    def _():
        _store_key_norm_max(k_ref, kmax_ref)

    q = [q_ref[0, hh] for hh in range(ATT_HEADS)]
    qk_bound = [_row_norm_max(q[hh]) * kmax_ref[hh][0:1, 0:1] + BOUND_SLACK for hh in range(ATT_HEADS)]
    lane = lax.broadcasted_iota(jnp.int32, (1, LANES), 1)
    bias_lanes = (lane >= HEAD_DIM) & (lane < HEAD_DIM + BIAS_SPLIT)
    tail = 16

    def older_bound(hh, g):
        rows = pl.ds(pl.multiple_of((g + 1) * ATT_GROUP - tail, tail), tail)
        bias = jnp.sum(jnp.where(bias_lanes, k_ref[0, hh, rows, :].astype(F32), 0.0), axis=1, keepdims=True)
        return qk_bound[hh] + jnp.max(bias, axis=0, keepdims=True)

    outs = _flash_heads(q, k_ref, v_ref, i, older_bound=older_bound)
    _store_heads(o_ref, outs)


def _fox_attention(qkv, k_aug):
    in_specs, out_spec = _attention_specs(HEAD_SLOT["fox_q"], 0, HEAD_SLOT["fox_v"])
    return pl.pallas_call(
        _fox_kernel,
        grid=(BATCH, N_HEADS // ATT_HEADS, N_ATT_TILES),
        in_specs=in_specs,
        out_specs=out_spec,
        out_shape=jax.ShapeDtypeStruct((BATCH, SEQ, WIDTH), BF16),
        scratch_shapes=[pltpu.VMEM((ATT_HEADS, 8, LANES), F32)],
        compiler_params=pltpu.CompilerParams(
            dimension_semantics=("arbitrary", "arbitrary", "arbitrary"), vmem_limit_bytes=VMEM_LIMIT),
        name="fox_attention",
    )(qkv, k_aug, qkv)


def _swa_kernel(sinks_ref, slopes_ref, q_ref, k_ref, v_ref, o_ref):
    hkv = pl.program_id(1)
    i = pl.program_id(2)
    w = SWA_WINDOW
    qi = lax.broadcasted_iota(jnp.int32, (w, 2 * w), 0)
    ki = lax.broadcasted_iota(jnp.int32, (w, 2 * w), 1)

    for sub in range(SWA_TQ // w):
        q_start = i * SWA_TQ + sub * w
        k_start = pl.multiple_of(jnp.maximum(q_start - w, 0), w)
        ks = k_ref[0, 0, pl.ds(k_start, 2 * w), :]
        vs = v_ref[0, 0, pl.ds(k_start, 2 * w), :]
        rel = (q_start - k_start) + qi - ki
        valid = (rel >= 0) & (rel < w)
        rel_f = rel.astype(F32)
        for g in range(SWA_GROUP):
            h = hkv * SWA_GROUP + g
            q = q_ref[0, g, sub * w:(sub + 1) * w, :] * jnp.asarray(SCALE, BF16)
            s = _dot_nt(q, ks) - slopes_ref[h] * rel_f
            s = jnp.where(valid, s, NEG_INF)
            sink = sinks_ref[h]
            m = jnp.maximum(jnp.max(s, axis=1, keepdims=True), sink)
            e = jnp.exp(s - m)
            denom = jnp.sum(e, axis=1, keepdims=True) + jnp.exp(sink - m)
            out = _dot((e / denom).astype(BF16), vs)[:, 0:HEAD_DIM]
            o_ref[0, sub * w:(sub + 1) * w, g * HEAD_DIM:(g + 1) * HEAD_DIM] = out.astype(BF16)


def _swa_attention(qkv, sinks, slopes):
    q0 = HEAD_SLOT["swa_q"] // SWA_GROUP
    k0, v0 = HEAD_SLOT["swa_k"], HEAD_SLOT["swa_v"]
    smem = pl.BlockSpec(memory_space=pltpu.SMEM)
    return pl.pallas_call(
        _swa_kernel,
        grid=(BATCH, SWA_KV_HEADS, SEQ // SWA_TQ),
        in_specs=[
            smem, smem,
            pl.BlockSpec((1, SWA_GROUP, SWA_TQ, LANES), lambda b, hk, i: (b, q0 + hk, i, 0)),
            pl.BlockSpec((1, 1, SEQ, LANES), lambda b, hk, i: (b, k0 + hk, 0, 0)),
            pl.BlockSpec((1, 1, SEQ, LANES), lambda b, hk, i: (b, v0 + hk, 0, 0)),
        ],
        out_specs=pl.BlockSpec((1, SWA_TQ, SWA_GROUP * HEAD_DIM), lambda b, hk, i: (b, i, hk)),
        out_shape=jax.ShapeDtypeStruct((BATCH, SEQ, WIDTH), BF16),
        compiler_params=pltpu.CompilerParams(
            dimension_semantics=("arbitrary", "arbitrary", "arbitrary"), vmem_limit_bytes=VMEM_LIMIT),
        name="swa_attention",
    )(sinks, slopes, qkv, qkv, qkv)


def _moba_kernel(slope2_ref, q_ref, k_ref, v_ref, o_ref, kmean_ref, onehot_ref, kmax_ref):
    hg = pl.program_id(1)
    i = pl.program_id(2)
    t = ATT_T
    blocks_per_tile = t // MOBA_BLOCK
    row_block = lax.broadcasted_iota(jnp.int32, (t, LANES), 0) // MOBA_BLOCK
    blk = lax.broadcasted_iota(jnp.int32, (t, LANES), 1)

    @pl.when(i == 0)
    def _():
        _store_key_norm_max(k_ref, kmax_ref)
        lane = lax.broadcasted_iota(jnp.int32, (1, LANES), 1)
        for hh in range(ATT_HEADS):
            def block_mean(bk, _):
                start = pl.multiple_of(bk * MOBA_BLOCK, MOBA_BLOCK)
                kb = k_ref[0, hh, pl.ds(start, MOBA_BLOCK), :].astype(F32)
                mean = jnp.sum(kb, axis=0, keepdims=True) * (1.0 / MOBA_BLOCK)
                kmean_ref[hh, pl.ds(bk, 1), :] = jnp.where(lane < HEAD_DIM, mean, 0.0)
                return 0
            lax.fori_loop(0, N_MOBA_BLOCKS, block_mean, 0)

        def indicator(j, _):
            rows = pl.ds(pl.multiple_of(j * t, t), t)
            onehot_ref[rows, :] = (blk == blocks_per_tile * j + row_block).astype(BF16)
            return 0
        lax.fori_loop(0, N_ATT_TILES, indicator, 0)

    blk_t = lax.broadcasted_iota(jnp.int32, (N_MOBA_BLOCKS, t), 0)
    own_t = blocks_per_tile * i + lax.broadcasted_iota(jnp.int32, (N_MOBA_BLOCKS, t), 1) // MOBA_BLOCK
    past = blk_t < own_t
    blk_f = blk_t.astype(F32)
    never = jnp.full((LANES - N_MOBA_BLOCKS, t), NEG_INF, F32)
    q_aug = []
    for hh in range(ATT_HEADS):
        q = q_ref[0, hh]
        gate = _dot_nt(kmean_ref[hh].astype(BF16), q)
        gate = jnp.where(past, gate, NEG_INF)
        sel = blk_t == own_t
        for _ in range(MOBA_TOPK):
            mx = jnp.max(gate, axis=0, keepdims=True)
            first = jnp.min(jnp.where(gate == mx, blk_f, float(N_MOBA_BLOCKS)), axis=0, keepdims=True)
            pick = blk_f == first
            sel = sel | (pick & past)
            gate = jnp.where(pick, -jnp.inf, gate)
        select_bias = jnp.concatenate([jnp.where(sel, 0.0, NEG_INF), never], axis=0)
        q_aug.append(jnp.concatenate([q, select_bias.T.astype(BF16)], axis=1))

    def group_offset(g):
        keys_back = ((i // (ATT_GROUP // t) - g) * ATT_GROUP).astype(F32)
        return [-(slope2_ref[ATT_HEADS * hg + hh] * keys_back) for hh in range(ATT_HEADS)]

    qk_bound = [_row_norm_max(q_ref[0, hh]) * kmax_ref[hh][0:1, 0:1] + BOUND_SLACK for hh in range(ATT_HEADS)]

    def older_bound(hh, g):
        return qk_bound[hh] + group_offset(g)[hh]

    outs = _flash_heads(q_aug, k_ref, v_ref, i, group_offset=group_offset, older_bound=older_bound,
                        key_extra=lambda start, width: onehot_ref[pl.ds(start, width), :])
    _store_heads(o_ref, outs)


def _moba_attention(qkv, slope2):
    in_specs, out_spec = _attention_specs(
        HEAD_SLOT["moba_q"], HEAD_SLOT["moba_k"] // ATT_HEADS, HEAD_SLOT["moba_v"])
    return pl.pallas_call(
        _moba_kernel,
        grid=(BATCH, N_HEADS // ATT_HEADS, N_ATT_TILES),
        in_specs=[pl.BlockSpec(memory_space=pltpu.SMEM)] + in_specs,
        out_specs=out_spec,
        out_shape=jax.ShapeDtypeStruct((BATCH, SEQ, WIDTH), BF16),
        scratch_shapes=[pltpu.VMEM((ATT_HEADS, N_MOBA_BLOCKS, LANES), F32),
                        pltpu.VMEM((SEQ, LANES), BF16),
                        pltpu.VMEM((ATT_HEADS, 8, LANES), F32)],
        compiler_params=pltpu.CompilerParams(
            dimension_semantics=("arbitrary", "arbitrary", "arbitrary"), vmem_limit_bytes=VMEM_LIMIT),
        name="moba_attention",
    )(slope2, qkv, qkv, qkv)


def _merge_kernel(x_ref, zg_ref, ofox_ref, oswa_ref, omoba_ref, wbr_ref, wout_ref, g_ref, b_ref, o_ref):
    y = jnp.zeros((MERGE_TM, D_MODEL), F32)
    for br, o_br in enumerate((ofox_ref, oswa_ref, omoba_ref)):
        silu_z = zg_ref[:, br * WIDTH:(br + 1) * WIDTH].astype(F32)
        a = (o_br[...].astype(F32) * silu_z).astype(BF16)
        gate = zg_ref[:, 3 * WIDTH + br * D_MODEL:3 * WIDTH + (br + 1) * D_MODEL].astype(F32)
        y = y + gate * _dot(a, wbr_ref[br])
    out = _dot(y.astype(BF16), wout_ref[...])
    r = DEEPNORM_ALPHA * x_ref[...] + out
    mu = jnp.mean(r, axis=1, keepdims=True)
    d = r - mu
    var = jnp.mean(d * d, axis=1, keepdims=True)
    o_ref[...] = d * lax.rsqrt(var + LN_EPS) * g_ref[...] + b_ref[...]


def _merge(x2, zg, o_fox, o_swa, o_moba, w_br, w_out, ln_g, ln_b):
    row_tile = lambda n: pl.BlockSpec((MERGE_TM, n), lambda i: (i, 0))
    fixed = lambda shape: pl.BlockSpec(shape, lambda i: (0,) * len(shape), pipeline_mode=pl.Buffered(1))
    return pl.pallas_call(
        _merge_kernel,
        grid=(ROWS // MERGE_TM,),
        in_specs=[
            row_tile(D_MODEL), row_tile(N_ZG), row_tile(WIDTH), row_tile(WIDTH), row_tile(WIDTH),
            fixed((3, WIDTH, D_MODEL)), fixed((D_MODEL, D_MODEL)), fixed((1, D_MODEL)), fixed((1, D_MODEL)),
        ],
        out_specs=row_tile(D_MODEL),
        out_shape=jax.ShapeDtypeStruct((ROWS, D_MODEL), F32),
        compiler_params=pltpu.CompilerParams(
            dimension_semantics=("arbitrary",), vmem_limit_bytes=VMEM_LIMIT),
        name="merge_deepnorm",
    )(x2, zg, o_fox, o_swa, o_moba, w_br, w_out, ln_g, ln_b)


def _columns(w, names):
    return jnp.concatenate([w[..., _OFF[n][0]:_OFF[n][1]] for n in names], axis=-1)


def _alibi_slopes(n):
    return jnp.power(2.0, -8.0 * jnp.arange(1, n + 1, dtype=F32) / n)


def _bias_lane_tables():
    cs = jnp.ones((N_QKV_HEADS, HEAD_DIM), F32)
    padc = jnp.zeros((N_QKV_HEADS, LANES), F32)
    rc1 = jnp.zeros((N_QKV_HEADS, LANES), F32)
    rc2 = jnp.zeros((N_QKV_HEADS, LANES), F32)
    heads = lambda name: slice(HEAD_SLOT[name], HEAD_SLOT[name] + N_HEADS)
    b0 = HEAD_DIM
    s3 = BIAS_SPLIT
    cs = cs.at[heads("fox_q")].set(SCALE * LOG2E).at[heads("moba_q")].set(SCALE * LOG2E)
    padc = padc.at[heads("fox_q"), b0:b0 + s3].set(1.0)
    padc = padc.at[heads("fox_v"), ONES_LANE].set(1.0).at[heads("moba_v"), ONES_LANE].set(1.0)
    slope_pieces = jnp.stack(_split3(_alibi_slopes(N_HEADS) * LOG2E), axis=1)
    padc = padc.at[heads("moba_q"), b0:b0 + s3].set(slope_pieces)
    padc = padc.at[heads("moba_q"), b0 + s3:b0 + 2 * s3].set(slope_pieces)
    rc1 = rc1.at[heads("moba_k"), b0:b0 + s3].set(1.0)
    rc2 = rc2.at[heads("moba_k"), b0 + s3:b0 + 2 * s3].set(1.0)
    flat = lambda a: a.reshape(1, -1)
    return flat(cs), flat(padc), flat(rc1), flat(rc2)


def _fox_place_matrix():
    place = jnp.zeros((BIAS_SPLIT * F_PAD, N_HEADS * LANES), F32)
    for p in range(BIAS_SPLIT):
        for h in range(N_HEADS):
            place = place.at[p * F_PAD + h, h * LANES + HEAD_DIM + p].set(-1.0)
    return place.astype(BF16)


def _layer(x2, w_in, b_in, sinks, w_br, w_out, ln_g, ln_b, tables, place):
    zg_names = ("fox_z", "swa_z", "moba_z", "gate_fox", "gate_swa", "gate_moba")
    w_qkv = _columns(w_in, QKV_ORDER).astype(BF16)
    b_qkv = _columns(b_in, QKV_ORDER)[None, :]
    w_f = jnp.pad(_columns(w_in, ("fox_f",)), ((0, 0), (0, F_PAD - N_HEADS))).astype(BF16)
    b_f = jnp.pad(_columns(b_in, ("fox_f",)), (0, F_PAD - N_HEADS))[None, :]
    w_zg = _columns(w_in, zg_names).astype(BF16)
    b_zg = _columns(b_in, zg_names)[None, :]

    qkv, f_raw = _qkv_proj(x2, w_qkv, b_qkv, *tables, w_f, b_f)
    zg = _zg_proj(x2, w_zg, b_zg)
    k_fox = _fox_pack(f_raw, qkv, place)
    o_fox = _fox_attention(qkv, k_fox).reshape(ROWS, WIDTH)
    o_swa = _swa_attention(qkv, sinks, _alibi_slopes(N_HEADS)).reshape(ROWS, WIDTH)
    o_moba = _moba_attention(qkv, _alibi_slopes(N_HEADS) * LOG2E).reshape(ROWS, WIDTH)
    return _merge(x2, zg, o_fox, o_swa, o_moba, w_br.astype(BF16), w_out.astype(BF16),
                  ln_g[None, :], ln_b[None, :])


def kernel(x, w_in, b_in, swa_sinks, w_branch_fox, w_branch_swa, w_branch_moba, w_out, ln_gain, ln_bias):
    x2 = x.reshape(ROWS, D_MODEL)
    tables = _bias_lane_tables()
    place = _fox_place_matrix()
    for l in range(DEPTH):
        w_br = jnp.stack([w_branch_fox[l], w_branch_swa[l], w_branch_moba[l]])
        x2 = _layer(x2, w_in[l], b_in[l], swa_sinks[l], w_br, w_out[l], ln_gain[l], ln_bias[l], tables, place)
    return x2.reshape(BATCH, SEQ, D_MODEL)
```

```python
import math

import jax
import jax.numpy as jnp
from jax import lax
from jax.experimental import pallas as pl
from jax.experimental.pallas import tpu as pltpu

D_MODEL = 2048
BATCH = 2
SEQ = 16384
DEPTH = 2
HEAD_DIM = 64
LANES = 128
N_HEADS = 8
SWA_KV_HEADS = 2
SWA_GROUP = N_HEADS // SWA_KV_HEADS
WIDTH = N_HEADS * HEAD_DIM
SWA_WINDOW = 128
MOBA_BLOCK = 256
MOBA_TOPK = 3
N_MOBA_BLOCKS = SEQ // MOBA_BLOCK
DEEPNORM_ALPHA = (2.0 * DEPTH) ** 0.25
LN_EPS = 1e-5
NEG_INF = -1e30
SCALE = HEAD_DIM ** -0.5
LOG2E = math.log2(math.e)

_SEG = (("fox_q", WIDTH), ("fox_k", WIDTH), ("fox_v", WIDTH), ("fox_z", WIDTH), ("fox_f", N_HEADS),
        ("swa_q", WIDTH), ("swa_k", SWA_KV_HEADS * HEAD_DIM), ("swa_v", SWA_KV_HEADS * HEAD_DIM), ("swa_z", WIDTH),
        ("moba_q", WIDTH), ("moba_k", WIDTH), ("moba_v", WIDTH), ("moba_z", WIDTH),
        ("gate_fox", D_MODEL), ("gate_swa", D_MODEL), ("gate_moba", D_MODEL))
_OFF = {}
_start = 0
for _name, _size in _SEG:
    _OFF[_name] = (_start, _start + _size)
    _start += _size

QKV_ORDER = ("fox_q", "fox_k", "fox_v", "swa_q", "swa_k", "swa_v", "moba_q", "moba_k", "moba_v")
HEAD_SLOT = {}
_slot = 0
for _name in QKV_ORDER:
    HEAD_SLOT[_name] = _slot
    _slot += (_OFF[_name][1] - _OFF[_name][0]) // HEAD_DIM
N_QKV_HEADS = _slot
N_QKV = N_QKV_HEADS * HEAD_DIM
N_ZG = 3 * WIDTH + 3 * D_MODEL
F_PAD = 128

ROWS = BATCH * SEQ
PROJ_TM = 1024
QKV_TN = 1280
ZG_TN = 1536
ATT_T = 512
ATT_GROUP = 1024
ATT_HEADS = 4
NORM_CHUNK = 2048
N_ATT_TILES = SEQ // ATT_T
ROW_BIAS_HEADS = frozenset(
    (HEAD_SLOT["moba_k"] + h) % (QKV_TN // HEAD_DIM) for h in range(N_HEADS))
SWA_TQ = 512
MERGE_TM = 512
VMEM_LIMIT = 56 * 1024 * 1024

BIAS_SPLIT = 3
ONES_LANE = HEAD_DIM
UNDERFLOW_BITS = 152.0
BOUND_SLACK = 8.0

F32 = jnp.float32
BF16 = jnp.bfloat16


def _dot(a, b):
    return jnp.dot(a, b, preferred_element_type=F32)


def _dot_nt(a, b):
    return lax.dot_general(a, b, (((1,), (1,)), ((), ())), preferred_element_type=F32)


def _sigmoid(v):
    return 0.5 * jnp.tanh(0.5 * v) + 0.5


def _split3(v):
    hi = v.astype(BF16).astype(F32)
    rem = v - hi
    mid = rem.astype(BF16).astype(F32)
    lo = (rem - mid).astype(BF16).astype(F32)
    return hi, mid, lo


def _qkv_proj_kernel(x_ref, w_ref, b_ref, cs_ref, padc_ref, rc1_ref, rc2_ref, wf_ref, bf_ref,
                     o_ref, f_ref, xb_ref):
    j = pl.program_id(1)

    @pl.when(j == 0)
    def _():
        xb_ref[...] = x_ref[...].astype(BF16)
        f_ref[...] = _dot(xb_ref[...], wf_ref[...]) + bf_ref[...]

    acc = (_dot(xb_ref[...], w_ref[...]) + b_ref[...]) * cs_ref[...]
    r = lax.broadcasted_iota(jnp.int32, (PROJ_TM, 1), 0)
    row1 = ((r % MOBA_BLOCK) - (MOBA_BLOCK - 1)).astype(F32)
    row2 = (((r // MOBA_BLOCK) % (ATT_GROUP // MOBA_BLOCK)) * MOBA_BLOCK - (ATT_GROUP - MOBA_BLOCK)).astype(F32)
    lane = lax.broadcasted_iota(jnp.int32, (PROJ_TM, LANES), 1)
    data_lane = lane < HEAD_DIM
    for pair in range(QKV_TN // LANES):
        both = acc[:, pair * LANES:(pair + 1) * LANES]
        swapped = pltpu.roll(both, HEAD_DIM, axis=1)
        for hh, data in ((2 * pair, both), (2 * pair + 1, swapped)):
            cols = slice(hh * LANES, (hh + 1) * LANES)
            bias = padc_ref[:, cols]
            if hh in ROW_BIAS_HEADS:
                bias = bias + rc1_ref[:, cols] * row1 + rc2_ref[:, cols] * row2
            o_ref[0, hh] = jnp.where(data_lane, data, bias).astype(BF16)


def _qkv_proj(x2, w, b, cs, padc, rc1, rc2, wf, bf):
    tiles_per_batch = SEQ // PROJ_TM
    heads_per_step = QKV_TN // HEAD_DIM
    col = lambda n: pl.BlockSpec((1, n), lambda i, j: (0, j))
    return pl.pallas_call(
        _qkv_proj_kernel,
        grid=(ROWS // PROJ_TM, N_QKV // QKV_TN),
        in_specs=[
            pl.BlockSpec((PROJ_TM, D_MODEL), lambda i, j: (i, 0)),
            pl.BlockSpec((D_MODEL, QKV_TN), lambda i, j: (0, j)),
            col(QKV_TN), col(QKV_TN),
            col(heads_per_step * LANES), col(heads_per_step * LANES), col(heads_per_step * LANES),
            pl.BlockSpec((D_MODEL, F_PAD), lambda i, j: (0, 0)),
            pl.BlockSpec((1, F_PAD), lambda i, j: (0, 0)),
        ],
        out_specs=[
            pl.BlockSpec((1, heads_per_step, PROJ_TM, LANES),
                         lambda i, j: (i // tiles_per_batch, j, i % tiles_per_batch, 0)),
            pl.BlockSpec((PROJ_TM, F_PAD), lambda i, j: (i, 0)),
        ],
        out_shape=[
            jax.ShapeDtypeStruct((BATCH, N_QKV_HEADS, SEQ, LANES), BF16),
            jax.ShapeDtypeStruct((ROWS, F_PAD), F32),
        ],
        scratch_shapes=[pltpu.VMEM((PROJ_TM, D_MODEL), BF16)],
        compiler_params=pltpu.CompilerParams(
            dimension_semantics=("arbitrary", "arbitrary"), vmem_limit_bytes=VMEM_LIMIT),
        name="qkv_proj",
    )(x2, w, b, cs, padc, rc1, rc2, wf, bf)


def _zg_proj_kernel(x_ref, w_ref, b_ref, o_ref, xb_ref):
    j = pl.program_id(1)

    @pl.when(j == 0)
    def _():
        xb_ref[...] = x_ref[...].astype(BF16)

    acc = _dot(xb_ref[...], w_ref[...]) + b_ref[...]
    is_z = j < (3 * WIDTH) // ZG_TN
    o_ref[...] = (jnp.where(is_z, acc, 1.0) * _sigmoid(acc)).astype(BF16)


def _zg_proj(x2, w, b):
    return pl.pallas_call(
        _zg_proj_kernel,
        grid=(ROWS // PROJ_TM, N_ZG // ZG_TN),
        in_specs=[
            pl.BlockSpec((PROJ_TM, D_MODEL), lambda i, j: (i, 0)),
            pl.BlockSpec((D_MODEL, ZG_TN), lambda i, j: (0, j)),
            pl.BlockSpec((1, ZG_TN), lambda i, j: (0, j)),
        ],
        out_specs=pl.BlockSpec((PROJ_TM, ZG_TN), lambda i, j: (i, j)),
        out_shape=jax.ShapeDtypeStruct((ROWS, N_ZG), BF16),
        scratch_shapes=[pltpu.VMEM((PROJ_TM, D_MODEL), BF16)],
        compiler_params=pltpu.CompilerParams(
            dimension_semantics=("arbitrary", "arbitrary"), vmem_limit_bytes=VMEM_LIMIT),
        name="zg_proj",
    )(x2, w, b)


def _fox_pack_kernel(f_ref, k_ref, place_ref, o_ref, carry_ref):
    t = pl.program_id(1)

    @pl.when(t == 0)
    def _():
        carry_ref[...] = jnp.zeros_like(carry_ref)

    f = f_ref[...]
    log_f = jnp.minimum(f, 0.0) - jnp.log(1.0 + jnp.exp(-jnp.abs(f)))
    row = lax.broadcasted_iota(jnp.int32, (ATT_T, ATT_T), 0)
    col = lax.broadcasted_iota(jnp.int32, (ATT_T, ATT_T), 1)
    tri = (row >= col).astype(BF16)
    pieces = jnp.concatenate([p.astype(BF16) for p in _split3(log_f)], axis=1)
    sums = _dot(tri, pieces)
    cum = (sums[:, 0:F_PAD] + sums[:, F_PAD:2 * F_PAD] + sums[:, 2 * F_PAD:3 * F_PAD]) + carry_ref[...]
    carry_ref[...] = cum[ATT_T - 1:ATT_T, :]
    c_pieces = jnp.concatenate([p.astype(BF16) for p in _split3(cum * LOG2E)], axis=1)
    placed = _dot(c_pieces, place_ref[...])
    for h in range(N_HEADS):
        o_ref[0, h] = (k_ref[0, h].astype(F32) + placed[:, h * LANES:(h + 1) * LANES]).astype(BF16)


def _fox_pack(f_raw, qkv, place):
    k_block = HEAD_SLOT["fox_k"] // N_HEADS
    return pl.pallas_call(
        _fox_pack_kernel,
        grid=(BATCH, N_ATT_TILES),
        in_specs=[
            pl.BlockSpec((ATT_T, F_PAD), lambda b, t: (b * N_ATT_TILES + t, 0)),
            pl.BlockSpec((1, N_HEADS, ATT_T, LANES), lambda b, t: (b, k_block, t, 0)),
            pl.BlockSpec((BIAS_SPLIT * F_PAD, N_HEADS * LANES), lambda b, t: (0, 0)),
        ],
        out_specs=pl.BlockSpec((1, N_HEADS, ATT_T, LANES), lambda b, t: (b, 0, t, 0)),
        out_shape=jax.ShapeDtypeStruct((BATCH, N_HEADS, SEQ, LANES), BF16),
        scratch_shapes=[pltpu.VMEM((1, F_PAD), F32)],
        compiler_params=pltpu.CompilerParams(dimension_semantics=("arbitrary", "arbitrary")),
        name="fox_pack",
    )(f_raw, qkv, place)


def _row_sq_norm_max(x):
    lane = lax.broadcasted_iota(jnp.int32, (1, LANES), 1)
    xf = jnp.where(lane < HEAD_DIM, x.astype(F32), 0.0)
    return jnp.max(jnp.sum(xf * xf, axis=1, keepdims=True), axis=0, keepdims=True)


def _row_norm_max(x):
    return jnp.sqrt(_row_sq_norm_max(x))


def _store_key_norm_max(k_ref, kmax_ref):
    for hh in range(ATT_HEADS):
        def chunk(c, best):
            rows = pl.ds(pl.multiple_of(c * NORM_CHUNK, NORM_CHUNK), NORM_CHUNK)
            return jnp.maximum(best, _row_sq_norm_max(k_ref[0, hh, rows, :]))
        best = lax.fori_loop(0, SEQ // NORM_CHUNK, chunk, jnp.zeros((1, 1), F32))
        kmax_ref[hh] = jnp.broadcast_to(jnp.sqrt(best), kmax_ref.shape[1:])


def _flash_heads(q_aug, k_ref, v_ref, i, key_extra=None, group_offset=None, older_bound=None):
    t = ATT_T

    def step(start, width, group, carry, causal):
        offs = None if group_offset is None else group_offset(group)
        extra = None if key_extra is None else key_extra(start, width)
        scores = []
        for hh in range(ATT_HEADS):
            ks = k_ref[0, hh, pl.ds(start, width), :]
            if extra is not None:
                ks = jnp.concatenate([ks, extra], axis=1)
            scores.append(_dot_nt(q_aug[hh], ks))
        new = []
        for hh in range(ATT_HEADS):
            m, acc = carry[hh]
            s = scores[hh]
            if causal:
                row = lax.broadcasted_iota(jnp.int32, (t, width), 0)
                col = lax.broadcasted_iota(jnp.int32, (t, width), 1)
                s = jnp.where(col <= row, s, NEG_INF)
            tile_max = jnp.max(s, axis=1, keepdims=True)
            if offs is not None:
                tile_max = tile_max + offs[hh]
            m_new = jnp.maximum(m, tile_max)
            shift = m_new if offs is None else m_new - offs[hh]
            alpha = jnp.exp2(m - m_new)
            p = jnp.exp2(s - shift).astype(BF16)
            acc = alpha * acc + _dot(p, v_ref[0, hh, pl.ds(start, width), :])
            new.append((m_new, acc))
        return tuple(new)

    own_group = i // (ATT_GROUP // t)
    carry = tuple((jnp.full((t, 1), NEG_INF, F32), jnp.zeros((t, LANES), F32)) for _ in range(ATT_HEADS))
    carry = step(pl.multiple_of(i * t, t), t, own_group, carry, True)
    carry = lax.fori_loop(
        own_group * (ATT_GROUP // t), i,
        lambda j, c: step(pl.multiple_of(j * t, t), t, own_group, c, False), carry)

    lowest_max = [jnp.min(m, axis=0, keepdims=True) for m, _ in carry]

    def still_needed(g):
        if older_bound is None:
            return jnp.int32(1)
        worst = None
        for hh in range(ATT_HEADS):
            gap = older_bound(hh, jnp.maximum(g, 0)) - lowest_max[hh]
            worst = gap if worst is None else jnp.maximum(worst, gap)
        return (jnp.max(worst) > -UNDERFLOW_BITS).astype(jnp.int32)

    def older_group(state):
        g, _, c = state
        needed_next = still_needed(g - 1)
        c = step(pl.multiple_of(g * ATT_GROUP, ATT_GROUP), ATT_GROUP, g, c, False)
        return g - 1, needed_next, c

    _, _, carry = lax.while_loop(
        lambda state: jnp.logical_and(state[0] >= 0, state[1] == 1), older_group,
        (own_group - 1, still_needed(own_group - 1), carry))
    return [acc / acc[:, ONES_LANE:ONES_LANE + 1] for _, acc in carry]


def _store_heads(o_ref, outs):
    lane = lax.broadcasted_iota(jnp.int32, (ATT_T, LANES), 1)
    for pair in range(ATT_HEADS // 2):
        both = jnp.where(lane < HEAD_DIM, outs[2 * pair], pltpu.roll(outs[2 * pair + 1], HEAD_DIM, axis=1))
        o_ref[0, :, pair * LANES:(pair + 1) * LANES] = both.astype(BF16)


def _attention_specs(q_slot, k_index, v_slot):
    q0, v0 = q_slot // ATT_HEADS, v_slot // ATT_HEADS
    resident = lambda index: pl.BlockSpec((1, ATT_HEADS, SEQ, LANES), index, pipeline_mode=pl.Buffered(1))
    in_specs = [
        pl.BlockSpec((1, ATT_HEADS, ATT_T, LANES), lambda b, hg, i: (b, q0 + hg, i, 0)),
        resident(lambda b, hg, i: (b, k_index + hg, 0, 0)),
        resident(lambda b, hg, i: (b, v0 + hg, 0, 0)),
    ]
    out_spec = pl.BlockSpec((1, ATT_T, ATT_HEADS * HEAD_DIM), lambda b, hg, i: (b, i, hg))
    return in_specs, out_spec


def _fox_kernel(q_ref, k_ref, v_ref, o_ref, kmax_ref):
    i = pl.program_id(2)

    @pl.when(i == 0)
    def _():
        _store_key_norm_max(k_ref, kmax_ref)

    q = [q_ref[0, hh] for hh in range(ATT_HEADS)]
    qk_bound = [_row_norm_max(q[hh]) * kmax_ref[hh][0:1, 0:1] + BOUND_SLACK for hh in range(ATT_HEADS)]
    lane = lax.broadcasted_iota(jnp.int32, (1, LANES), 1)
    bias_lanes = (lane >= HEAD_DIM) & (lane < HEAD_DIM + BIAS_SPLIT)
    tail = 16

    def older_bound(hh, g):
        rows = pl.ds(pl.multiple_of((g + 1) * ATT_GROUP - tail, tail), tail)
        bias = jnp.sum(jnp.where(bias_lanes, k_ref[0, hh, rows, :].astype(F32), 0.0), axis=1, keepdims=True)
        return qk_bound[hh] + jnp.max(bias, axis=0, keepdims=True)

    outs = _flash_heads(q, k_ref, v_ref, i, older_bound=older_bound)
    _store_heads(o_ref, outs)


def _fox_attention(qkv, k_aug):
    in_specs, out_spec = _attention_specs(HEAD_SLOT["fox_q"], 0, HEAD_SLOT["fox_v"])
    return pl.pallas_call(
        _fox_kernel,
        grid=(BATCH, N_HEADS // ATT_HEADS, N_ATT_TILES),
        in_specs=in_specs,
        out_specs=out_spec,
        out_shape=jax.ShapeDtypeStruct((BATCH, SEQ, WIDTH), BF16),
        scratch_shapes=[pltpu.VMEM((ATT_HEADS, 8, LANES), F32)],
        compiler_params=pltpu.CompilerParams(
            dimension_semantics=("arbitrary", "arbitrary", "arbitrary"), vmem_limit_bytes=VMEM_LIMIT),
        name="fox_attention",
    )(qkv, k_aug, qkv)


def _swa_kernel(sinks_ref, slopes_ref, q_ref, k_ref, v_ref, o_ref):
    hkv = pl.program_id(1)
    i = pl.program_id(2)
    w = SWA_WINDOW
    qi = lax.broadcasted_iota(jnp.int32, (w, 2 * w), 0)
    ki = lax.broadcasted_iota(jnp.int32, (w, 2 * w), 1)

    for sub in range(SWA_TQ // w):
        q_start = i * SWA_TQ + sub * w
        k_start = pl.multiple_of(jnp.maximum(q_start - w, 0), w)
        ks = k_ref[0, 0, pl.ds(k_start, 2 * w), :]
        vs = v_ref[0, 0, pl.ds(k_start, 2 * w), :]
        rel = (q_start - k_start) + qi - ki
        valid = (rel >= 0) & (rel < w)
        rel_f = rel.astype(F32)
        for g in range(SWA_GROUP):
            h = hkv * SWA_GROUP + g
            q = q_ref[0, g, sub * w:(sub + 1) * w, :] * jnp.asarray(SCALE, BF16)
            s = _dot_nt(q, ks) - slopes_ref[h] * rel_f
            s = jnp.where(valid, s, NEG_INF)
            sink = sinks_ref[h]
            m = jnp.maximum(jnp.max(s, axis=1, keepdims=True), sink)
            e = jnp.exp(s - m)
            denom = jnp.sum(e, axis=1, keepdims=True) + jnp.exp(sink - m)
            out = _dot((e / denom).astype(BF16), vs)[:, 0:HEAD_DIM]
            o_ref[0, sub * w:(sub + 1) * w, g * HEAD_DIM:(g + 1) * HEAD_DIM] = out.astype(BF16)


def _swa_attention(qkv, sinks, slopes):
    q0 = HEAD_SLOT["swa_q"] // SWA_GROUP
    k0, v0 = HEAD_SLOT["swa_k"], HEAD_SLOT["swa_v"]
    smem = pl.BlockSpec(memory_space=pltpu.SMEM)
    return pl.pallas_call(
        _swa_kernel,
        grid=(BATCH, SWA_KV_HEADS, SEQ // SWA_TQ),
        in_specs=[
            smem, smem,
            pl.BlockSpec((1, SWA_GROUP, SWA_TQ, LANES), lambda b, hk, i: (b, q0 + hk, i, 0)),
            pl.BlockSpec((1, 1, SEQ, LANES), lambda b, hk, i: (b, k0 + hk, 0, 0)),
            pl.BlockSpec((1, 1, SEQ, LANES), lambda b, hk, i: (b, v0 + hk, 0, 0)),
        ],
        out_specs=pl.BlockSpec((1, SWA_TQ, SWA_GROUP * HEAD_DIM), lambda b, hk, i: (b, i, hk)),
        out_shape=jax.ShapeDtypeStruct((BATCH, SEQ, WIDTH), BF16),
        compiler_params=pltpu.CompilerParams(
            dimension_semantics=("arbitrary", "arbitrary", "arbitrary"), vmem_limit_bytes=VMEM_LIMIT),
        name="swa_attention",
    )(sinks, slopes, qkv, qkv, qkv)


def _moba_kernel(slope2_ref, q_ref, k_ref, v_ref, o_ref, kmean_ref, onehot_ref, kmax_ref):
    hg = pl.program_id(1)
    i = pl.program_id(2)
    t = ATT_T
    blocks_per_tile = t // MOBA_BLOCK
    row_block = lax.broadcasted_iota(jnp.int32, (t, LANES), 0) // MOBA_BLOCK
    blk = lax.broadcasted_iota(jnp.int32, (t, LANES), 1)

    @pl.when(i == 0)
    def _():
        _store_key_norm_max(k_ref, kmax_ref)
        lane = lax.broadcasted_iota(jnp.int32, (1, LANES), 1)
        for hh in range(ATT_HEADS):
            def block_mean(bk, _):
                start = pl.multiple_of(bk * MOBA_BLOCK, MOBA_BLOCK)
                kb = k_ref[0, hh, pl.ds(start, MOBA_BLOCK), :].astype(F32)
                mean = jnp.sum(kb, axis=0, keepdims=True) * (1.0 / MOBA_BLOCK)
                kmean_ref[hh, pl.ds(bk, 1), :] = jnp.where(lane < HEAD_DIM, mean, 0.0)
                return 0
            lax.fori_loop(0, N_MOBA_BLOCKS, block_mean, 0)

        def indicator(j, _):
            rows = pl.ds(pl.multiple_of(j * t, t), t)
            onehot_ref[rows, :] = (blk == blocks_per_tile * j + row_block).astype(BF16)
            return 0
        lax.fori_loop(0, N_ATT_TILES, indicator, 0)

    blk_t = lax.broadcasted_iota(jnp.int32, (N_MOBA_BLOCKS, t), 0)
    own_t = blocks_per_tile * i + lax.broadcasted_iota(jnp.int32, (N_MOBA_BLOCKS, t), 1) // MOBA_BLOCK
    past = blk_t < own_t
    blk_f = blk_t.astype(F32)
    never = jnp.full((LANES - N_MOBA_BLOCKS, t), NEG_INF, F32)
    q_aug = []
    for hh in range(ATT_HEADS):
        q = q_ref[0, hh]
        gate = _dot_nt(kmean_ref[hh].astype(BF16), q)
        gate = jnp.where(past, gate, NEG_INF)
        sel = blk_t == own_t
        for _ in range(MOBA_TOPK):
            mx = jnp.max(gate, axis=0, keepdims=True)
            first = jnp.min(jnp.where(gate == mx, blk_f, float(N_MOBA_BLOCKS)), axis=0, keepdims=True)
            pick = blk_f == first
            sel = sel | (pick & past)
            gate = jnp.where(pick, -jnp.inf, gate)
        select_bias = jnp.concatenate([jnp.where(sel, 0.0, NEG_INF), never], axis=0)
        q_aug.append(jnp.concatenate([q, select_bias.T.astype(BF16)], axis=1))

    def group_offset(g):
        keys_back = ((i // (ATT_GROUP // t) - g) * ATT_GROUP).astype(F32)
        return [-(slope2_ref[ATT_HEADS * hg + hh] * keys_back) for hh in range(ATT_HEADS)]

    qk_bound = [_row_norm_max(q_ref[0, hh]) * kmax_ref[hh][0:1, 0:1] + BOUND_SLACK for hh in range(ATT_HEADS)]

    def older_bound(hh, g):
        return qk_bound[hh] + group_offset(g)[hh]

    outs = _flash_heads(q_aug, k_ref, v_ref, i, group_offset=group_offset, older_bound=older_bound,
                        key_extra=lambda start, width: onehot_ref[pl.ds(start, width), :])
    _store_heads(o_ref, outs)


def _moba_attention(qkv, slope2):
    in_specs, out_spec = _attention_specs(
        HEAD_SLOT["moba_q"], HEAD_SLOT["moba_k"] // ATT_HEADS, HEAD_SLOT["moba_v"])
    return pl.pallas_call(
        _moba_kernel,
        grid=(BATCH, N_HEADS // ATT_HEADS, N_ATT_TILES),
        in_specs=[pl.BlockSpec(memory_space=pltpu.SMEM)] + in_specs,
        out_specs=out_spec,
        out_shape=jax.ShapeDtypeStruct((BATCH, SEQ, WIDTH), BF16),
        scratch_shapes=[pltpu.VMEM((ATT_HEADS, N_MOBA_BLOCKS, LANES), F32),
                        pltpu.VMEM((SEQ, LANES), BF16),
                        pltpu.VMEM((ATT_HEADS, 8, LANES), F32)],
        compiler_params=pltpu.CompilerParams(
            dimension_semantics=("arbitrary", "arbitrary", "arbitrary"), vmem_limit_bytes=VMEM_LIMIT),
        name="moba_attention",
    )(slope2, qkv, qkv, qkv)


def _merge_kernel(x_ref, zg_ref, ofox_ref, oswa_ref, omoba_ref, wbr_ref, wout_ref, g_ref, b_ref, o_ref):
    y = jnp.zeros((MERGE_TM, D_MODEL), F32)
    for br, o_br in enumerate((ofox_ref, oswa_ref, omoba_ref)):
        silu_z = zg_ref[:, br * WIDTH:(br + 1) * WIDTH].astype(F32)
        a = (o_br[...].astype(F32) * silu_z).astype(BF16)
        gate = zg_ref[:, 3 * WIDTH + br * D_MODEL:3 * WIDTH + (br + 1) * D_MODEL].astype(F32)
        y = y + gate * _dot(a, wbr_ref[br])
    out = _dot(y.astype(BF16), wout_ref[...])
    r = DEEPNORM_ALPHA * x_ref[...] + out
    mu = jnp.mean(r, axis=1, keepdims=True)
    d = r - mu
    var = jnp.mean(d * d, axis=1, keepdims=True)
    o_ref[...] = d * lax.rsqrt(var + LN_EPS) * g_ref[...] + b_ref[...]


def _merge(x2, zg, o_fox, o_swa, o_moba, w_br, w_out, ln_g, ln_b):
    row_tile = lambda n: pl.BlockSpec((MERGE_TM, n), lambda i: (i, 0))
    fixed = lambda shape: pl.BlockSpec(shape, lambda i: (0,) * len(shape), pipeline_mode=pl.Buffered(1))
    return pl.pallas_call(
        _merge_kernel,
        grid=(ROWS // MERGE_TM,),
        in_specs=[
            row_tile(D_MODEL), row_tile(N_ZG), row_tile(WIDTH), row_tile(WIDTH), row_tile(WIDTH),
            fixed((3, WIDTH, D_MODEL)), fixed((D_MODEL, D_MODEL)), fixed((1, D_MODEL)), fixed((1, D_MODEL)),
        ],
        out_specs=row_tile(D_MODEL),
        out_shape=jax.ShapeDtypeStruct((ROWS, D_MODEL), F32),
        compiler_params=pltpu.CompilerParams(
            dimension_semantics=("arbitrary",), vmem_limit_bytes=VMEM_LIMIT),
        name="merge_deepnorm",
    )(x2, zg, o_fox, o_swa, o_moba, w_br, w_out, ln_g, ln_b)


def _columns(w, names):
    return jnp.concatenate([w[..., _OFF[n][0]:_OFF[n][1]] for n in names], axis=-1)


def _alibi_slopes(n):
    return jnp.power(2.0, -8.0 * jnp.arange(1, n + 1, dtype=F32) / n)


def _bias_lane_tables():
    cs = jnp.ones((N_QKV_HEADS, HEAD_DIM), F32)
    padc = jnp.zeros((N_QKV_HEADS, LANES), F32)
    rc1 = jnp.zeros((N_QKV_HEADS, LANES), F32)
    rc2 = jnp.zeros((N_QKV_HEADS, LANES), F32)
    heads = lambda name: slice(HEAD_SLOT[name], HEAD_SLOT[name] + N_HEADS)
    b0 = HEAD_DIM
    s3 = BIAS_SPLIT
    cs = cs.at[heads("fox_q")].set(SCALE * LOG2E).at[heads("moba_q")].set(SCALE * LOG2E)
    padc = padc.at[heads("fox_q"), b0:b0 + s3].set(1.0)
    padc = padc.at[heads("fox_v"), ONES_LANE].set(1.0).at[heads("moba_v"), ONES_LANE].set(1.0)
    slope_pieces = jnp.stack(_split3(_alibi_slopes(N_HEADS) * LOG2E), axis=1)
    padc = padc.at[heads("moba_q"), b0:b0 + s3].set(slope_pieces)
    padc = padc.at[heads("moba_q"), b0 + s3:b0 + 2 * s3].set(slope_pieces)
    rc1 = rc1.at[heads("moba_k"), b0:b0 + s3].set(1.0)
    rc2 = rc2.at[heads("moba_k"), b0 + s3:b0 + 2 * s3].set(1.0)
    flat = lambda a: a.reshape(1, -1)
    return flat(cs), flat(padc), flat(rc1), flat(rc2)


def _fox_place_matrix():
    place = jnp.zeros((BIAS_SPLIT * F_PAD, N_HEADS * LANES), F32)
    for p in range(BIAS_SPLIT):
        for h in range(N_HEADS):
            place = place.at[p * F_PAD + h, h * LANES + HEAD_DIM + p].set(-1.0)
    return place.astype(BF16)


def _layer(x2, w_in, b_in, sinks, w_br, w_out, ln_g, ln_b, tables, place):
    zg_names = ("fox_z", "swa_z", "moba_z", "gate_fox", "gate_swa", "gate_moba")
    w_qkv = _columns(w_in, QKV_ORDER).astype(BF16)
    b_qkv = _columns(b_in, QKV_ORDER)[None, :]
    w_f = jnp.pad(_columns(w_in, ("fox_f",)), ((0, 0), (0, F_PAD - N_HEADS))).astype(BF16)
    b_f = jnp.pad(_columns(b_in, ("fox_f",)), (0, F_PAD - N_HEADS))[None, :]
    w_zg = _columns(w_in, zg_names).astype(BF16)
    b_zg = _columns(b_in, zg_names)[None, :]

    qkv, f_raw = _qkv_proj(x2, w_qkv, b_qkv, *tables, w_f, b_f)
    zg = _zg_proj(x2, w_zg, b_zg)
    k_fox = _fox_pack(f_raw, qkv, place)
    o_fox = _fox_attention(qkv, k_fox).reshape(ROWS, WIDTH)
    o_swa = _swa_attention(qkv, sinks, _alibi_slopes(N_HEADS)).reshape(ROWS, WIDTH)
    o_moba = _moba_attention(qkv, _alibi_slopes(N_HEADS) * LOG2E).reshape(ROWS, WIDTH)
    return _merge(x2, zg, o_fox, o_swa, o_moba, w_br.astype(BF16), w_out.astype(BF16),
                  ln_g[None, :], ln_b[None, :])


def kernel(x, w_in, b_in, swa_sinks, w_branch_fox, w_branch_swa, w_branch_moba, w_out, ln_gain, ln_bias):
    x2 = x.reshape(ROWS, D_MODEL)
    tables = _bias_lane_tables()
    place = _fox_place_matrix()
    for l in range(DEPTH):
        w_br = jnp.stack([w_branch_fox[l], w_branch_swa[l], w_branch_moba[l]])
        x2 = _layer(x2, w_in[l], b_in[l], swa_sinks[l], w_br, w_out[l], ln_gain[l], ln_bias[l], tables, place)
    return x2.reshape(BATCH, SEQ, D_MODEL)
```

```python
import math

import jax
import jax.numpy as jnp
from jax import lax
from jax.experimental import pallas as pl
from jax.experimental.pallas import tpu as pltpu

D_MODEL = 2048
BATCH = 2
SEQ = 16384
DEPTH = 2
HEAD_DIM = 64
LANES = 128
N_HEADS = 8
SWA_KV_HEADS = 2
SWA_GROUP = N_HEADS // SWA_KV_HEADS
WIDTH = N_HEADS * HEAD_DIM
SWA_WINDOW = 128
MOBA_BLOCK = 256
MOBA_TOPK = 3
N_MOBA_BLOCKS = SEQ // MOBA_BLOCK
DEEPNORM_ALPHA = (2.0 * DEPTH) ** 0.25
LN_EPS = 1e-5
NEG_INF = -1e30
SCALE = HEAD_DIM ** -0.5
LOG2E = math.log2(math.e)

_SEG = (("fox_q", WIDTH), ("fox_k", WIDTH), ("fox_v", WIDTH), ("fox_z", WIDTH), ("fox_f", N_HEADS),
        ("swa_q", WIDTH), ("swa_k", SWA_KV_HEADS * HEAD_DIM), ("swa_v", SWA_KV_HEADS * HEAD_DIM), ("swa_z", WIDTH),
        ("moba_q", WIDTH), ("moba_k", WIDTH), ("moba_v", WIDTH), ("moba_z", WIDTH),
        ("gate_fox", D_MODEL), ("gate_swa", D_MODEL), ("gate_moba", D_MODEL))
_OFF = {}
_start = 0
for _name, _size in _SEG:
    _OFF[_name] = (_start, _start + _size)
    _start += _size

QKV_ORDER = ("fox_q", "fox_k", "fox_v", "swa_q", "swa_k", "swa_v", "moba_q", "moba_k", "moba_v")
HEAD_SLOT = {}
_slot = 0
for _name in QKV_ORDER:
    HEAD_SLOT[_name] = _slot
    _slot += (_OFF[_name][1] - _OFF[_name][0]) // HEAD_DIM
N_QKV_HEADS = _slot
N_QKV = N_QKV_HEADS * HEAD_DIM
N_ZG = 3 * WIDTH + 3 * D_MODEL
F_PAD = 128

ROWS = BATCH * SEQ
PROJ_TM = 1024
QKV_TN = 1280
ZG_TN = 1536
ATT_T = 1024
ATT_GROUP = 1024
ATT_HEADS = 2
NORM_CHUNK = 2048
N_ATT_TILES = SEQ // ATT_T
ROW_BIAS_HEADS = frozenset(
    (HEAD_SLOT["moba_k"] + h) % (QKV_TN // HEAD_DIM) for h in range(N_HEADS))
SWA_TQ = 512
MERGE_TM = 512
VMEM_LIMIT = 56 * 1024 * 1024

BIAS_SPLIT = 3
ONES_LANE = HEAD_DIM
UNDERFLOW_BITS = 152.0
BOUND_SLACK = 8.0

F32 = jnp.float32
BF16 = jnp.bfloat16


def _dot(a, b):
    return jnp.dot(a, b, preferred_element_type=F32)


def _dot_nt(a, b):
    return lax.dot_general(a, b, (((1,), (1,)), ((), ())), preferred_element_type=F32)


def _sigmoid(v):
    return 0.5 * jnp.tanh(0.5 * v) + 0.5


def _split3(v):
    hi = v.astype(BF16).astype(F32)
    rem = v - hi
    mid = rem.astype(BF16).astype(F32)
    lo = (rem - mid).astype(BF16).astype(F32)
    return hi, mid, lo


def _qkv_proj_kernel(x_ref, w_ref, b_ref, cs_ref, padc_ref, rc1_ref, rc2_ref, wf_ref, bf_ref,
                     o_ref, f_ref, xb_ref):
    j = pl.program_id(1)

    @pl.when(j == 0)
    def _():
        xb_ref[...] = x_ref[...].astype(BF16)
        f_ref[...] = _dot(xb_ref[...], wf_ref[...]) + bf_ref[...]

    acc = (_dot(xb_ref[...], w_ref[...]) + b_ref[...]) * cs_ref[...]
    r = lax.broadcasted_iota(jnp.int32, (PROJ_TM, 1), 0)
    row1 = ((r % MOBA_BLOCK) - (MOBA_BLOCK - 1)).astype(F32)
    row2 = (((r // MOBA_BLOCK) % (ATT_GROUP // MOBA_BLOCK)) * MOBA_BLOCK - (ATT_GROUP - MOBA_BLOCK)).astype(F32)
    lane = lax.broadcasted_iota(jnp.int32, (PROJ_TM, LANES), 1)
    data_lane = lane < HEAD_DIM
    for pair in range(QKV_TN // LANES):
        both = acc[:, pair * LANES:(pair + 1) * LANES]
        swapped = pltpu.roll(both, HEAD_DIM, axis=1)
        for hh, data in ((2 * pair, both), (2 * pair + 1, swapped)):
            cols = slice(hh * LANES, (hh + 1) * LANES)
            bias = padc_ref[:, cols]
            if hh in ROW_BIAS_HEADS:
                bias = bias + rc1_ref[:, cols] * row1 + rc2_ref[:, cols] * row2
            o_ref[0, hh] = jnp.where(data_lane, data, bias).astype(BF16)


def _qkv_proj(x2, w, b, cs, padc, rc1, rc2, wf, bf):
    tiles_per_batch = SEQ // PROJ_TM
    heads_per_step = QKV_TN // HEAD_DIM
    col = lambda n: pl.BlockSpec((1, n), lambda i, j: (0, j))
    return pl.pallas_call(
        _qkv_proj_kernel,
        grid=(ROWS // PROJ_TM, N_QKV // QKV_TN),
        in_specs=[
            pl.BlockSpec((PROJ_TM, D_MODEL), lambda i, j: (i, 0)),
            pl.BlockSpec((D_MODEL, QKV_TN), lambda i, j: (0, j)),
            col(QKV_TN), col(QKV_TN),
            col(heads_per_step * LANES), col(heads_per_step * LANES), col(heads_per_step * LANES),
            pl.BlockSpec((D_MODEL, F_PAD), lambda i, j: (0, 0)),
            pl.BlockSpec((1, F_PAD), lambda i, j: (0, 0)),
        ],
        out_specs=[
            pl.BlockSpec((1, heads_per_step, PROJ_TM, LANES),
                         lambda i, j: (i // tiles_per_batch, j, i % tiles_per_batch, 0)),
            pl.BlockSpec((PROJ_TM, F_PAD), lambda i, j: (i, 0)),
        ],
        out_shape=[
            jax.ShapeDtypeStruct((BATCH, N_QKV_HEADS, SEQ, LANES), BF16),
            jax.ShapeDtypeStruct((ROWS, F_PAD), F32),
        ],
        scratch_shapes=[pltpu.VMEM((PROJ_TM, D_MODEL), BF16)],
        compiler_params=pltpu.CompilerParams(
            dimension_semantics=("arbitrary", "arbitrary"), vmem_limit_bytes=VMEM_LIMIT),
        name="qkv_proj",
    )(x2, w, b, cs, padc, rc1, rc2, wf, bf)


def _zg_proj_kernel(x_ref, w_ref, b_ref, o_ref, xb_ref):
    j = pl.program_id(1)

    @pl.when(j == 0)
    def _():
        xb_ref[...] = x_ref[...].astype(BF16)

    acc = _dot(xb_ref[...], w_ref[...]) + b_ref[...]
    is_z = j < (3 * WIDTH) // ZG_TN
    o_ref[...] = (jnp.where(is_z, acc, 1.0) * _sigmoid(acc)).astype(BF16)


def _zg_proj(x2, w, b):
    return pl.pallas_call(
        _zg_proj_kernel,
        grid=(ROWS // PROJ_TM, N_ZG // ZG_TN),
        in_specs=[
            pl.BlockSpec((PROJ_TM, D_MODEL), lambda i, j: (i, 0)),
            pl.BlockSpec((D_MODEL, ZG_TN), lambda i, j: (0, j)),
            pl.BlockSpec((1, ZG_TN), lambda i, j: (0, j)),
        ],
        out_specs=pl.BlockSpec((PROJ_TM, ZG_TN), lambda i, j: (i, j)),
        out_shape=jax.ShapeDtypeStruct((ROWS, N_ZG), BF16),
        scratch_shapes=[pltpu.VMEM((PROJ_TM, D_MODEL), BF16)],
        compiler_params=pltpu.CompilerParams(
            dimension_semantics=("arbitrary", "arbitrary"), vmem_limit_bytes=VMEM_LIMIT),
        name="zg_proj",
    )(x2, w, b)


def _fox_pack_kernel(f_ref, k_ref, place_ref, o_ref, carry_ref):
    t = pl.program_id(1)

    @pl.when(t == 0)
    def _():
        carry_ref[...] = jnp.zeros_like(carry_ref)

    f = f_ref[...]
    log_f = jnp.minimum(f, 0.0) - jnp.log(1.0 + jnp.exp(-jnp.abs(f)))
    row = lax.broadcasted_iota(jnp.int32, (ATT_T, ATT_T), 0)
    col = lax.broadcasted_iota(jnp.int32, (ATT_T, ATT_T), 1)
    tri = (row >= col).astype(BF16)
    pieces = jnp.concatenate([p.astype(BF16) for p in _split3(log_f)], axis=1)
    sums = _dot(tri, pieces)
    cum = (sums[:, 0:F_PAD] + sums[:, F_PAD:2 * F_PAD] + sums[:, 2 * F_PAD:3 * F_PAD]) + carry_ref[...]
    carry_ref[...] = cum[ATT_T - 1:ATT_T, :]
    c_pieces = jnp.concatenate([p.astype(BF16) for p in _split3(cum * LOG2E)], axis=1)
    placed = _dot(c_pieces, place_ref[...])
    for h in range(N_HEADS):
        o_ref[0, h] = (k_ref[0, h].astype(F32) + placed[:, h * LANES:(h + 1) * LANES]).astype(BF16)


def _fox_pack(f_raw, qkv, place):
    k_block = HEAD_SLOT["fox_k"] // N_HEADS
    return pl.pallas_call(
        _fox_pack_kernel,
        grid=(BATCH, N_ATT_TILES),
        in_specs=[
            pl.BlockSpec((ATT_T, F_PAD), lambda b, t: (b * N_ATT_TILES + t, 0)),
            pl.BlockSpec((1, N_HEADS, ATT_T, LANES), lambda b, t: (b, k_block, t, 0)),
            pl.BlockSpec((BIAS_SPLIT * F_PAD, N_HEADS * LANES), lambda b, t: (0, 0)),
        ],
        out_specs=pl.BlockSpec((1, N_HEADS, ATT_T, LANES), lambda b, t: (b, 0, t, 0)),
        out_shape=jax.ShapeDtypeStruct((BATCH, N_HEADS, SEQ, LANES), BF16),
        scratch_shapes=[pltpu.VMEM((1, F_PAD), F32)],
        compiler_params=pltpu.CompilerParams(dimension_semantics=("arbitrary", "arbitrary")),
        name="fox_pack",
    )(f_raw, qkv, place)


def _row_sq_norm_max(x):
    lane = lax.broadcasted_iota(jnp.int32, (1, LANES), 1)
    xf = jnp.where(lane < HEAD_DIM, x.astype(F32), 0.0)
    return jnp.max(jnp.sum(xf * xf, axis=1, keepdims=True), axis=0, keepdims=True)


def _row_norm_max(x):
    return jnp.sqrt(_row_sq_norm_max(x))


def _store_key_norm_max(k_ref, kmax_ref):
    for hh in range(ATT_HEADS):
        def chunk(c, best):
            rows = pl.ds(pl.multiple_of(c * NORM_CHUNK, NORM_CHUNK), NORM_CHUNK)
            return jnp.maximum(best, _row_sq_norm_max(k_ref[0, hh, rows, :]))
        best = lax.fori_loop(0, SEQ // NORM_CHUNK, chunk, jnp.zeros((1, 1), F32))
        kmax_ref[hh] = jnp.broadcast_to(jnp.sqrt(best), kmax_ref.shape[1:])


def _flash_heads(q_aug, k_ref, v_ref, i, key_extra=None, group_offset=None, older_bound=None):
    t = ATT_T

    def step(start, width, group, carry, causal):
        offs = None if group_offset is None else group_offset(group)
        extra = None if key_extra is None else key_extra(start, width)
        scores = []
        for hh in range(ATT_HEADS):
            ks = k_ref[0, hh, pl.ds(start, width), :]
            if extra is not None:
                ks = jnp.concatenate([ks, extra], axis=1)
            scores.append(_dot_nt(q_aug[hh], ks))
        new = []
        for hh in range(ATT_HEADS):
            m, acc = carry[hh]
            s = scores[hh]
            if causal:
                row = lax.broadcasted_iota(jnp.int32, (t, width), 0)
                col = lax.broadcasted_iota(jnp.int32, (t, width), 1)
                s = jnp.where(col <= row, s, NEG_INF)
            tile_max = jnp.max(s, axis=1, keepdims=True)
            if offs is not None:
                tile_max = tile_max + offs[hh]
            m_new = jnp.maximum(m, tile_max)
            shift = m_new if offs is None else m_new - offs[hh]
            alpha = jnp.exp2(m - m_new)
            p = jnp.exp2(s - shift).astype(BF16)
            acc = alpha * acc + _dot(p, v_ref[0, hh, pl.ds(start, width), :])
            new.append((m_new, acc))
        return tuple(new)

    own_group = i // (ATT_GROUP // t)
    carry = tuple((jnp.full((t, 1), NEG_INF, F32), jnp.zeros((t, LANES), F32)) for _ in range(ATT_HEADS))
    carry = step(pl.multiple_of(i * t, t), t, own_group, carry, True)
    carry = lax.fori_loop(
        own_group * (ATT_GROUP // t), i,
        lambda j, c: step(pl.multiple_of(j * t, t), t, own_group, c, False), carry)

    lowest_max = [jnp.min(m, axis=0, keepdims=True) for m, _ in carry]

    def still_needed(g):
        if older_bound is None:
            return jnp.int32(1)
        worst = None
        for hh in range(ATT_HEADS):
            gap = older_bound(hh, jnp.maximum(g, 0)) - lowest_max[hh]
            worst = gap if worst is None else jnp.maximum(worst, gap)
        return (jnp.max(worst) > -UNDERFLOW_BITS).astype(jnp.int32)

    def older_group(state):
        g, _, c = state
        needed_next = still_needed(g - 1)
        c = step(pl.multiple_of(g * ATT_GROUP, ATT_GROUP), ATT_GROUP, g, c, False)
        return g - 1, needed_next, c

    _, _, carry = lax.while_loop(
        lambda state: jnp.logical_and(state[0] >= 0, state[1] == 1), older_group,
        (own_group - 1, still_needed(own_group - 1), carry))
    return [acc / acc[:, ONES_LANE:ONES_LANE + 1] for _, acc in carry]


def _store_heads(o_ref, outs):
    lane = lax.broadcasted_iota(jnp.int32, (ATT_T, LANES), 1)
    for pair in range(ATT_HEADS // 2):
        both = jnp.where(lane < HEAD_DIM, outs[2 * pair], pltpu.roll(outs[2 * pair + 1], HEAD_DIM, axis=1))
        o_ref[0, :, pair * LANES:(pair + 1) * LANES] = both.astype(BF16)


def _attention_specs(q_slot, k_index, v_slot):
    q0, v0 = q_slot // ATT_HEADS, v_slot // ATT_HEADS
    resident = lambda index: pl.BlockSpec((1, ATT_HEADS, SEQ, LANES), index, pipeline_mode=pl.Buffered(1))
    in_specs = [
        pl.BlockSpec((1, ATT_HEADS, ATT_T, LANES), lambda b, hg, i: (b, q0 + hg, i, 0)),
        resident(lambda b, hg, i: (b, k_index + hg, 0, 0)),
        resident(lambda b, hg, i: (b, v0 + hg, 0, 0)),
    ]
    out_spec = pl.BlockSpec((1, ATT_T, ATT_HEADS * HEAD_DIM), lambda b, hg, i: (b, i, hg))
    return in_specs, out_spec


def _fox_kernel(q_ref, k_ref, v_ref, o_ref, kmax_ref):
    i = pl.program_id(2)

    @pl.when(i == 0)
    def _():
        _store_key_norm_max(k_ref, kmax_ref)

    q = [q_ref[0, hh] for hh in range(ATT_HEADS)]
    qk_bound = [_row_norm_max(q[hh]) * kmax_ref[hh][0:1, 0:1] + BOUND_SLACK for hh in range(ATT_HEADS)]
    lane = lax.broadcasted_iota(jnp.int32, (1, LANES), 1)
    bias_lanes = (lane >= HEAD_DIM) & (lane < HEAD_DIM + BIAS_SPLIT)
    tail = 16

    def older_bound(hh, g):
        rows = pl.ds(pl.multiple_of((g + 1) * ATT_GROUP - tail, tail), tail)
        bias = jnp.sum(jnp.where(bias_lanes, k_ref[0, hh, rows, :].astype(F32), 0.0), axis=1, keepdims=True)
        return qk_bound[hh] + jnp.max(bias, axis=0, keepdims=True)

    outs = _flash_heads(q, k_ref, v_ref, i, older_bound=older_bound)
    _store_heads(o_ref, outs)


def _fox_attention(qkv, k_aug):
    in_specs, out_spec = _attention_specs(HEAD_SLOT["fox_q"], 0, HEAD_SLOT["fox_v"])
    return pl.pallas_call(
        _fox_kernel,
        grid=(BATCH, N_HEADS // ATT_HEADS, N_ATT_TILES),
        in_specs=in_specs,
        out_specs=out_spec,
        out_shape=jax.ShapeDtypeStruct((BATCH, SEQ, WIDTH), BF16),
        scratch_shapes=[pltpu.VMEM((ATT_HEADS, 8, LANES), F32)],
        compiler_params=pltpu.CompilerParams(
            dimension_semantics=("arbitrary", "arbitrary", "arbitrary"), vmem_limit_bytes=VMEM_LIMIT),
        name="fox_attention",
    )(qkv, k_aug, qkv)


def _swa_kernel(sinks_ref, slopes_ref, q_ref, k_ref, v_ref, o_ref):
    hkv = pl.program_id(1)
    i = pl.program_id(2)
    w = SWA_WINDOW
    qi = lax.broadcasted_iota(jnp.int32, (w, 2 * w), 0)
    ki = lax.broadcasted_iota(jnp.int32, (w, 2 * w), 1)

    for sub in range(SWA_TQ // w):
        q_start = i * SWA_TQ + sub * w
        k_start = pl.multiple_of(jnp.maximum(q_start - w, 0), w)
        ks = k_ref[0, 0, pl.ds(k_start, 2 * w), :]
        vs = v_ref[0, 0, pl.ds(k_start, 2 * w), :]
        rel = (q_start - k_start) + qi - ki
        valid = (rel >= 0) & (rel < w)
        rel_f = rel.astype(F32)
        for g in range(SWA_GROUP):
            h = hkv * SWA_GROUP + g
            q = q_ref[0, g, sub * w:(sub + 1) * w, :] * jnp.asarray(SCALE, BF16)
            s = _dot_nt(q, ks) - slopes_ref[h] * rel_f
            s = jnp.where(valid, s, NEG_INF)
            sink = sinks_ref[h]
            m = jnp.maximum(jnp.max(s, axis=1, keepdims=True), sink)
            e = jnp.exp(s - m)
            denom = jnp.sum(e, axis=1, keepdims=True) + jnp.exp(sink - m)
            out = _dot((e / denom).astype(BF16), vs)[:, 0:HEAD_DIM]
            o_ref[0, sub * w:(sub + 1) * w, g * HEAD_DIM:(g + 1) * HEAD_DIM] = out.astype(BF16)


def _swa_attention(qkv, sinks, slopes):
    q0 = HEAD_SLOT["swa_q"] // SWA_GROUP
    k0, v0 = HEAD_SLOT["swa_k"], HEAD_SLOT["swa_v"]
    smem = pl.BlockSpec(memory_space=pltpu.SMEM)
    return pl.pallas_call(
        _swa_kernel,
        grid=(BATCH, SWA_KV_HEADS, SEQ // SWA_TQ),
        in_specs=[
            smem, smem,
            pl.BlockSpec((1, SWA_GROUP, SWA_TQ, LANES), lambda b, hk, i: (b, q0 + hk, i, 0)),
            pl.BlockSpec((1, 1, SEQ, LANES), lambda b, hk, i: (b, k0 + hk, 0, 0)),
            pl.BlockSpec((1, 1, SEQ, LANES), lambda b, hk, i: (b, v0 + hk, 0, 0)),
        ],
        out_specs=pl.BlockSpec((1, SWA_TQ, SWA_GROUP * HEAD_DIM), lambda b, hk, i: (b, i, hk)),
        out_shape=jax.ShapeDtypeStruct((BATCH, SEQ, WIDTH), BF16),
        compiler_params=pltpu.CompilerParams(
            dimension_semantics=("arbitrary", "arbitrary", "arbitrary"), vmem_limit_bytes=VMEM_LIMIT),
        name="swa_attention",
    )(sinks, slopes, qkv, qkv, qkv)


def _moba_kernel(slope2_ref, q_ref, k_ref, v_ref, o_ref, kmean_ref, onehot_ref, kmax_ref):
    hg = pl.program_id(1)
    i = pl.program_id(2)
    t = ATT_T
    blocks_per_tile = t // MOBA_BLOCK
    row_block = lax.broadcasted_iota(jnp.int32, (t, LANES), 0) // MOBA_BLOCK
    blk = lax.broadcasted_iota(jnp.int32, (t, LANES), 1)

    @pl.when(i == 0)
    def _():
        _store_key_norm_max(k_ref, kmax_ref)
        lane = lax.broadcasted_iota(jnp.int32, (1, LANES), 1)
        for hh in range(ATT_HEADS):
            def block_mean(bk, _):
                start = pl.multiple_of(bk * MOBA_BLOCK, MOBA_BLOCK)
                kb = k_ref[0, hh, pl.ds(start, MOBA_BLOCK), :].astype(F32)
                mean = jnp.sum(kb, axis=0, keepdims=True) * (1.0 / MOBA_BLOCK)
                kmean_ref[hh, pl.ds(bk, 1), :] = jnp.where(lane < HEAD_DIM, mean, 0.0)
                return 0
            lax.fori_loop(0, N_MOBA_BLOCKS, block_mean, 0)

        def indicator(j, _):
            rows = pl.ds(pl.multiple_of(j * t, t), t)
            onehot_ref[rows, :] = (blk == blocks_per_tile * j + row_block).astype(BF16)
            return 0
        lax.fori_loop(0, N_ATT_TILES, indicator, 0)

    blk_t = lax.broadcasted_iota(jnp.int32, (N_MOBA_BLOCKS, t), 0)
    own_t = blocks_per_tile * i + lax.broadcasted_iota(jnp.int32, (N_MOBA_BLOCKS, t), 1) // MOBA_BLOCK
    past = blk_t < own_t
    blk_f = blk_t.astype(F32)
    never = jnp.full((LANES - N_MOBA_BLOCKS, t), NEG_INF, F32)
    q_aug = []
    for hh in range(ATT_HEADS):
        q = q_ref[0, hh]
        gate = _dot_nt(kmean_ref[hh].astype(BF16), q)
        gate = jnp.where(past, gate, NEG_INF)
        sel = blk_t == own_t
        for _ in range(MOBA_TOPK):
            mx = jnp.max(gate, axis=0, keepdims=True)
            first = jnp.min(jnp.where(gate == mx, blk_f, float(N_MOBA_BLOCKS)), axis=0, keepdims=True)
            pick = blk_f == first
            sel = sel | (pick & past)
            gate = jnp.where(pick, -jnp.inf, gate)
        select_bias = jnp.concatenate([jnp.where(sel, 0.0, NEG_INF), never], axis=0)
        q_aug.append(jnp.concatenate([q, select_bias.T.astype(BF16)], axis=1))

    def group_offset(g):
        keys_back = ((i // (ATT_GROUP // t) - g) * ATT_GROUP).astype(F32)
        return [-(slope2_ref[ATT_HEADS * hg + hh] * keys_back) for hh in range(ATT_HEADS)]

    qk_bound = [_row_norm_max(q_ref[0, hh]) * kmax_ref[hh][0:1, 0:1] + BOUND_SLACK for hh in range(ATT_HEADS)]

    def older_bound(hh, g):
        return qk_bound[hh] + group_offset(g)[hh]

    outs = _flash_heads(q_aug, k_ref, v_ref, i, group_offset=group_offset, older_bound=older_bound,
                        key_extra=lambda start, width: onehot_ref[pl.ds(start, width), :])
    _store_heads(o_ref, outs)


def _moba_attention(qkv, slope2):
    in_specs, out_spec = _attention_specs(
        HEAD_SLOT["moba_q"], HEAD_SLOT["moba_k"] // ATT_HEADS, HEAD_SLOT["moba_v"])
    return pl.pallas_call(
        _moba_kernel,
        grid=(BATCH, N_HEADS // ATT_HEADS, N_ATT_TILES),
        in_specs=[pl.BlockSpec(memory_space=pltpu.SMEM)] + in_specs,
        out_specs=out_spec,
        out_shape=jax.ShapeDtypeStruct((BATCH, SEQ, WIDTH), BF16),
        scratch_shapes=[pltpu.VMEM((ATT_HEADS, N_MOBA_BLOCKS, LANES), F32),
                        pltpu.VMEM((SEQ, LANES), BF16),
                        pltpu.VMEM((ATT_HEADS, 8, LANES), F32)],
        compiler_params=pltpu.CompilerParams(
            dimension_semantics=("arbitrary", "arbitrary", "arbitrary"), vmem_limit_bytes=VMEM_LIMIT),
        name="moba_attention",
    )(slope2, qkv, qkv, qkv)


def _merge_kernel(x_ref, zg_ref, ofox_ref, oswa_ref, omoba_ref, wbr_ref, wout_ref, g_ref, b_ref, o_ref):
    y = jnp.zeros((MERGE_TM, D_MODEL), F32)
    for br, o_br in enumerate((ofox_ref, oswa_ref, omoba_ref)):
        silu_z = zg_ref[:, br * WIDTH:(br + 1) * WIDTH].astype(F32)
        a = (o_br[...].astype(F32) * silu_z).astype(BF16)
        gate = zg_ref[:, 3 * WIDTH + br * D_MODEL:3 * WIDTH + (br + 1) * D_MODEL].astype(F32)
        y = y + gate * _dot(a, wbr_ref[br])
    out = _dot(y.astype(BF16), wout_ref[...])
    r = DEEPNORM_ALPHA * x_ref[...] + out
    mu = jnp.mean(r, axis=1, keepdims=True)
    d = r - mu
    var = jnp.mean(d * d, axis=1, keepdims=True)
    o_ref[...] = d * lax.rsqrt(var + LN_EPS) * g_ref[...] + b_ref[...]


def _merge(x2, zg, o_fox, o_swa, o_moba, w_br, w_out, ln_g, ln_b):
    row_tile = lambda n: pl.BlockSpec((MERGE_TM, n), lambda i: (i, 0))
    fixed = lambda shape: pl.BlockSpec(shape, lambda i: (0,) * len(shape), pipeline_mode=pl.Buffered(1))
    return pl.pallas_call(
        _merge_kernel,
        grid=(ROWS // MERGE_TM,),
        in_specs=[
            row_tile(D_MODEL), row_tile(N_ZG), row_tile(WIDTH), row_tile(WIDTH), row_tile(WIDTH),
            fixed((3, WIDTH, D_MODEL)), fixed((D_MODEL, D_MODEL)), fixed((1, D_MODEL)), fixed((1, D_MODEL)),
        ],
        out_specs=row_tile(D_MODEL),
        out_shape=jax.ShapeDtypeStruct((ROWS, D_MODEL), F32),
        compiler_params=pltpu.CompilerParams(
            dimension_semantics=("arbitrary",), vmem_limit_bytes=VMEM_LIMIT),
        name="merge_deepnorm",
    )(x2, zg, o_fox, o_swa, o_moba, w_br, w_out, ln_g, ln_b)


def _columns(w, names):
    return jnp.concatenate([w[..., _OFF[n][0]:_OFF[n][1]] for n in names], axis=-1)


def _alibi_slopes(n):
    return jnp.power(2.0, -8.0 * jnp.arange(1, n + 1, dtype=F32) / n)


def _bias_lane_tables():
    cs = jnp.ones((N_QKV_HEADS, HEAD_DIM), F32)
    padc = jnp.zeros((N_QKV_HEADS, LANES), F32)
    rc1 = jnp.zeros((N_QKV_HEADS, LANES), F32)
    rc2 = jnp.zeros((N_QKV_HEADS, LANES), F32)
    heads = lambda name: slice(HEAD_SLOT[name], HEAD_SLOT[name] + N_HEADS)
    b0 = HEAD_DIM
    s3 = BIAS_SPLIT
    cs = cs.at[heads("fox_q")].set(SCALE * LOG2E).at[heads("moba_q")].set(SCALE * LOG2E)
    padc = padc.at[heads("fox_q"), b0:b0 + s3].set(1.0)
    padc = padc.at[heads("fox_v"), ONES_LANE].set(1.0).at[heads("moba_v"), ONES_LANE].set(1.0)
    slope_pieces = jnp.stack(_split3(_alibi_slopes(N_HEADS) * LOG2E), axis=1)
    padc = padc.at[heads("moba_q"), b0:b0 + s3].set(slope_pieces)
    padc = padc.at[heads("moba_q"), b0 + s3:b0 + 2 * s3].set(slope_pieces)
    rc1 = rc1.at[heads("moba_k"), b0:b0 + s3].set(1.0)
    rc2 = rc2.at[heads("moba_k"), b0 + s3:b0 + 2 * s3].set(1.0)
    flat = lambda a: a.reshape(1, -1)
    return flat(cs), flat(padc), flat(rc1), flat(rc2)


def _fox_place_matrix():
    place = jnp.zeros((BIAS_SPLIT * F_PAD, N_HEADS * LANES), F32)
    for p in range(BIAS_SPLIT):
        for h in range(N_HEADS):
            place = place.at[p * F_PAD + h, h * LANES + HEAD_DIM + p].set(-1.0)
    return place.astype(BF16)


def _layer(x2, w_in, b_in, sinks, w_br, w_out, ln_g, ln_b, tables, place):
    zg_names = ("fox_z", "swa_z", "moba_z", "gate_fox", "gate_swa", "gate_moba")
    w_qkv = _columns(w_in, QKV_ORDER).astype(BF16)
    b_qkv = _columns(b_in, QKV_ORDER)[None, :]
    w_f = jnp.pad(_columns(w_in, ("fox_f",)), ((0, 0), (0, F_PAD - N_HEADS))).astype(BF16)
    b_f = jnp.pad(_columns(b_in, ("fox_f",)), (0, F_PAD - N_HEADS))[None, :]
    w_zg = _columns(w_in, zg_names).astype(BF16)
    b_zg = _columns(b_in, zg_names)[None, :]

    qkv, f_raw = _qkv_proj(x2, w_qkv, b_qkv, *tables, w_f, b_f)
    zg = _zg_proj(x2, w_zg, b_zg)
    k_fox = _fox_pack(f_raw, qkv, place)
    o_fox = _fox_attention(qkv, k_fox).reshape(ROWS, WIDTH)
    o_swa = _swa_attention(qkv, sinks, _alibi_slopes(N_HEADS)).reshape(ROWS, WIDTH)
    o_moba = _moba_attention(qkv, _alibi_slopes(N_HEADS) * LOG2E).reshape(ROWS, WIDTH)
    return _merge(x2, zg, o_fox, o_swa, o_moba, w_br.astype(BF16), w_out.astype(BF16),
                  ln_g[None, :], ln_b[None, :])


def kernel(x, w_in, b_in, swa_sinks, w_branch_fox, w_branch_swa, w_branch_moba, w_out, ln_gain, ln_bias):
    x2 = x.reshape(ROWS, D_MODEL)
    tables = _bias_lane_tables()
    place = _fox_place_matrix()
    for l in range(DEPTH):
        w_br = jnp.stack([w_branch_fox[l], w_branch_swa[l], w_branch_moba[l]])
        x2 = _layer(x2, w_in[l], b_in[l], swa_sinks[l], w_br, w_out[l], ln_gain[l], ln_bias[l], tables, place)
    return x2.reshape(BATCH, SEQ, D_MODEL)
```
